```python
import jax, jax.numpy as jnp
from jax import lax
import numpy as np

D_MODEL = 2048
BATCH = 4
SEQ = 2048
DEPTH = 2
DEC_BATCH = 8
DEC_SEQ = 4
PAST_LEN = 16384
PAGE_SIZE = 128

HEAD_DIM = 128
N_META = 16
H_ATT = D_MODEL // 256
H_CONV = D_MODEL // 512
H_MLSTM = D_MODEL // 512
D_ATT = H_ATT * HEAD_DIM
D_CONV = H_CONV * HEAD_DIM
D_MLSTM = H_MLSTM * HEAD_DIM
D_MIX = D_ATT + D_CONV + D_MLSTM
CONV_WIDTH = 3
D_FF = 4 * D_MODEL
Q_BLOCK = 128
CHUNK = 128
D_IN = 3 * D_ATT + 3 * D_CONV + 4 * D_MLSTM + 2 * H_MLSTM
ALPHA = (2 * DEPTH) ** 0.25
BETA_INIT = (8 * DEPTH) ** -0.25
SB_SCALE = HEAD_DIM ** -0.5
SB_BIAS_INIT = -6.0
MLSTM_K_SCALE = HEAD_DIM ** -0.5
LN_EPS = 1e-5
RMS_EPS = 1e-6

kernel_name = 'hymba_sb_conv_mlstm_step'


def _split_points():
    sizes = [D_ATT] * 3 + [D_CONV] * 3 + [D_MLSTM] * 4 + [H_MLSTM] * 2
    pts, acc = [], 0
    for s in sizes[:-1]:
        acc += s
        pts.append(acc)
    return pts


def _heads(t):
    return t.reshape(t.shape[:-1] + (t.shape[-1] // HEAD_DIM, HEAD_DIM))


def layer_norm(x, g, b):
    xf = x.astype(jnp.float32)
    mu = jnp.mean(xf, axis=-1, keepdims=True)
    var = jnp.mean(jnp.square(xf - mu), axis=-1, keepdims=True)
    return ((xf - mu) * lax.rsqrt(var + LN_EPS) * g + b).astype(x.dtype)


def project(x, w_in, b_i, b_f):
    p = x @ w_in
    qa, ka, va, bc, cc, xc, qm, km, vm, om, im, fm = jnp.split(p, _split_points(), axis=-1)
    ig = (im + b_i).astype(jnp.float32)
    lf = jax.nn.log_sigmoid((fm + b_f).astype(jnp.float32))
    return (_heads(qa), _heads(ka), _heads(va), bc, cc * xc,
            _heads(qm), _heads(km) * MLSTM_K_SCALE, _heads(vm), _heads(jax.nn.sigmoid(om)), ig, lf)


def sb_attend(q, k, v, bias, q_pos, k_pos):
    z = (jnp.einsum('bqhd,bkhd->bhqk', q, k).astype(jnp.float32) * SB_SCALE
         + bias.astype(jnp.float32)[None, :, None, None])
    mask = k_pos[None, :] < q_pos[:, None]
    log_keep = jnp.where(mask, jax.nn.log_sigmoid(-z), 0.0)
    log_pass = lax.cumsum(log_keep, axis=3, reverse=True) - log_keep
    a = jnp.where(mask, jnp.exp(jax.nn.log_sigmoid(z) + log_pass), 0.0)
    return jnp.einsum('bhqk,bkhd->bqhd', a, v.astype(jnp.float32))


def short_conv(u_ext, w):
    t = u_ext.shape[1] - (CONV_WIDTH - 1)
    return sum(w[j] * u_ext[:, j:j + t] for j in range(CONV_WIDTH))


def mlstm_chunk(carry, xs):
    c0, n0, m0 = carry
    q, k, v, ig, lf = xs
    L = q.shape[1]
    cf = jnp.moveaxis(jnp.cumsum(lf, axis=1), 1, 2)
    igt = jnp.moveaxis(ig, 1, 2)
    causal = jnp.arange(L)[:, None] >= jnp.arange(L)[None, :]
    log_d = jnp.where(causal, cf[..., :, None] - cf[..., None, :] + igt[..., None, :], -jnp.inf)
    log_inter = cf + m0[..., None]
    m_t = jnp.maximum(log_inter, jnp.max(log_d, axis=-1))
    w = jnp.exp(log_d - m_t[..., None]) * jnp.einsum('bthd,bshd->bhts', q, k)
    s_inter = jnp.exp(log_inter - m_t)
    num = s_inter[..., None] * jnp.einsum('bhvk,bthk->bhtv', c0, q) + jnp.einsum('bhts,bshv->bhtv', w, v)
    den = s_inter * jnp.einsum('bhk,bthk->bht', n0, q) + jnp.sum(w, axis=-1)
    h = num / jnp.maximum(jnp.abs(den), jnp.exp(-m_t))[..., None]
    cf_last = cf[..., -1]
    log_w = cf_last[..., None] - cf + igt
    m_new = jnp.maximum(cf_last + m0, jnp.max(log_w, axis=-1))
    wk = jnp.exp(log_w - m_new[..., None])
    decay = jnp.exp(cf_last + m0 - m_new)
    c_new = decay[..., None, None] * c0 + jnp.einsum('bhs,bshv,bshk->bhvk', wk, v, k)
    n_new = decay[..., None] * n0 + jnp.einsum('bhs,bshk->bhk', wk, k)
    return (c_new, n_new, m_new), jnp.moveaxis(h, 1, 2)


def mlstm_prompt(q, k, v, ig, lf):
    b = q.shape[0]
    n_seq = q.shape[1] - N_META
    f32 = jnp.float32
    ins = (q.astype(f32), k.astype(f32), v.astype(f32), ig, lf)
    st = (jnp.zeros((b, H_MLSTM, HEAD_DIM, HEAD_DIM), f32),
          jnp.zeros((b, H_MLSTM, HEAD_DIM), f32),
          jnp.zeros((b, H_MLSTM), f32))
    st, h_meta = mlstm_chunk(st, tuple(t[:, :N_META] for t in ins))

    def to_chunks(t):
        t = t[:, N_META:]
        return jnp.moveaxis(t.reshape((b, n_seq // CHUNK, CHUNK) + t.shape[2:]), 1, 0)

    st, h_real = lax.scan(mlstm_chunk, st, tuple(to_chunks(t) for t in ins))
    h_real = jnp.moveaxis(h_real, 0, 1).reshape((b, n_seq) + h_real.shape[3:])
    return jnp.concatenate([h_meta, h_real], axis=1), st


def merge_and_mlp(x, h_att, y_conv, h_ml, g_mix, w_out, ln1_g, ln1_b, w_up, w_down, ln2_g, ln2_b):
    b, t = x.shape[:2]
    f32 = jnp.float32
    h = jnp.concatenate([h_att.astype(f32), y_conv.astype(f32).reshape(b, t, H_CONV, HEAD_DIM),
                         h_ml.astype(f32)], axis=2)
    h = h * lax.rsqrt(jnp.mean(h * h, axis=-1, keepdims=True) + RMS_EPS)
    h = (h.reshape(b, t, D_MIX) * g_mix).astype(x.dtype)
    x = layer_norm(ALPHA * x + h @ w_out, ln1_g, ln1_b)
    u = jax.nn.relu(x @ w_up)
    return layer_norm(ALPHA * x + (u * u) @ w_down, ln2_g, ln2_b)


def setup_inputs(seed: int = 0) -> dict:
    key = jax.random.key(seed)
    ks = jax.random.split(key, 23)
    f32 = jnp.float32
    n_pages = PAST_LEN // PAGE_SIZE
    n_phys = (DEC_BATCH * n_pages * 5) // 4

    def nrm(k, shape, scale=1.0):
        return jax.random.normal(k, shape, f32) * scale

    page_table = jax.random.permutation(ks[4], n_phys)[: DEC_BATCH * n_pages]
    page_table = page_table.reshape(DEC_BATCH, n_pages).astype(jnp.int32)
    return {
        'x_prompt': nrm(ks[0], (BATCH, SEQ, D_MODEL)),
        'x_sample': nrm(ks[1], (DEC_BATCH, DEC_SEQ, D_MODEL)),
        'cache_k': nrm(ks[2], (DEPTH, n_phys, PAGE_SIZE, H_ATT, HEAD_DIM)),
        'cache_v': nrm(ks[3], (DEPTH, n_phys, PAGE_SIZE, H_ATT, HEAD_DIM)),
        'page_table': page_table,
        'state_conv': nrm(ks[5], (DEPTH, DEC_BATCH, CONV_WIDTH - 1, D_CONV)),
        'state_C': nrm(ks[6], (DEPTH, DEC_BATCH, H_MLSTM, HEAD_DIM, HEAD_DIM), 0.1),
        'state_n': nrm(ks[7], (DEPTH, DEC_BATCH, H_MLSTM, HEAD_DIM), 0.1),
        'state_m': nrm(ks[8], (DEPTH, DEC_BATCH, H_MLSTM)),
        'meta_tokens': nrm(ks[9], (N_META, D_MODEL)),
        'w_in': nrm(ks[10], (DEPTH, D_MODEL, D_IN), D_MODEL ** -0.5),
        'sb_bias': SB_BIAS_INIT + nrm(ks[22], (DEPTH, H_ATT), 0.1),
        'conv_w': nrm(ks[11], (DEPTH, CONV_WIDTH, D_CONV), CONV_WIDTH ** -0.5),
        'b_i': nrm(ks[12], (DEPTH, H_MLSTM), 0.1),
        'b_f': 3.0 + nrm(ks[13], (DEPTH, H_MLSTM), 0.5),
        'g_mix': 1.0 + nrm(ks[14], (DEPTH, D_MIX), 0.05),
        'w_out': nrm(ks[15], (DEPTH, D_MIX, D_MODEL), BETA_INIT * D_MIX ** -0.5),
        'ln1_g': 1.0 + nrm(ks[16], (DEPTH, D_MODEL), 0.05),
        'ln1_b': nrm(ks[17], (DEPTH, D_MODEL), 0.02),
        'w_up': nrm(ks[18], (DEPTH, D_MODEL, D_FF), D_MODEL ** -0.5),
        'w_down': nrm(ks[19], (DEPTH, D_FF, D_MODEL), BETA_INIT * D_FF ** -0.5),
        'ln2_g': 1.0 + nrm(ks[20], (DEPTH, D_MODEL), 0.05),
        'ln2_b': nrm(ks[21], (DEPTH, D_MODEL), 0.02),
    }


def reference(x_prompt, x_sample, cache_k, cache_v, page_table, state_conv, state_C, state_n, state_m,
              meta_tokens, w_in, sb_bias, conv_w, b_i, b_f, g_mix, w_out, ln1_g, ln1_b, w_up, w_down,
              ln2_g, ln2_b):
    f32 = jnp.float32
    bp, n_seq = x_prompt.shape[0], x_prompt.shape[1]
    bs, n_dec = x_sample.shape[0], x_sample.shape[1]
    t_p = N_META + n_seq
    past = page_table.shape[1] * PAGE_SIZE
    xp = jnp.concatenate([jnp.broadcast_to(meta_tokens[None].astype(x_prompt.dtype), (bp, N_META, D_MODEL)),
                          x_prompt], axis=1)
    xs = x_sample
    k_pos_p = jnp.arange(t_p)
    q_pos_meta = jnp.arange(N_META)
    q_pos_blocks = (N_META + jnp.arange(n_seq)).reshape(n_seq // Q_BLOCK, Q_BLOCK)
    k_pos_s = jnp.arange(past + n_dec)
    q_pos_s = past + jnp.arange(n_dec)

    kp_l, vp_l, cp_l, Cp_l, np_l, mp_l = [], [], [], [], [], []
    ks_l, vs_l, cs_l, Cs_l, ns_l, ms_l = [], [], [], [], [], []
    for l in range(DEPTH):
        qa, ka, va, bc, u, qm, km, vm, om, ig, lf = project(xp, w_in[l], b_i[l], b_f[l])
        bias_l = sb_bias[l]
        att_meta = sb_attend(qa[:, :N_META], ka, va, bias_l, q_pos_meta, k_pos_p)
        q_blocks = jnp.moveaxis(qa[:, N_META:].reshape(bp, n_seq // Q_BLOCK, Q_BLOCK, H_ATT, HEAD_DIM), 1, 0)
        att_blocks = lax.map(lambda qb: sb_attend(qb[0], ka, va, bias_l, qb[1], k_pos_p), (q_blocks, q_pos_blocks))
        att_real = jnp.moveaxis(att_blocks, 0, 1).reshape(bp, n_seq, H_ATT, HEAD_DIM)
        h_att = jnp.concatenate([att_meta, att_real], axis=1)
        u_ext = jnp.concatenate([jnp.zeros((bp, CONV_WIDTH - 1, D_CONV), u.dtype), u], axis=1)
        y_conv = bc * short_conv(u_ext, conv_w[l])
        h_ml, (c_fin, n_fin, m_fin) = mlstm_prompt(qm, km, vm, ig, lf)
        kp_l.append(ka)
        vp_l.append(va)
        cp_l.append(u_ext[:, -(CONV_WIDTH - 1):])
        Cp_l.append(c_fin)
        np_l.append(n_fin)
        mp_l.append(m_fin)
        xp = merge_and_mlp(xp, h_att, y_conv, om * h_ml, g_mix[l], w_out[l],
                           ln1_g[l], ln1_b[l], w_up[l], w_down[l], ln2_g[l], ln2_b[l])

        qa, ka, va, bc, u, qm, km, vm, om, ig, lf = project(xs, w_in[l], b_i[l], b_f[l])
        k_past = cache_k[l][page_table].reshape(bs, past, H_ATT, HEAD_DIM)
        v_past = cache_v[l][page_table].reshape(bs, past, H_ATT, HEAD_DIM)
        k_all = jnp.concatenate([k_past, ka.astype(k_past.dtype)], axis=1)
        v_all = jnp.concatenate([v_past, va.astype(v_past.dtype)], axis=1)
        h_att = sb_attend(qa, k_all, v_all, bias_l, q_pos_s, k_pos_s)
        u_ext = jnp.concatenate([state_conv[l].astype(u.dtype), u], axis=1)
        y_conv = bc * short_conv(u_ext, conv_w[l])
        st0 = (state_C[l].astype(f32), state_n[l].astype(f32), state_m[l].astype(f32))
        (c_s, n_s, m_s), h_ml = mlstm_chunk(st0, (qm.astype(f32), km.astype(f32), vm.astype(f32), ig, lf))
        ks_l.append(ka)
        vs_l.append(va)
        cs_l.append(u_ext[:, -(CONV_WIDTH - 1):])
        Cs_l.append(c_s)
        ns_l.append(n_s)
        ms_l.append(m_s)
        xs = merge_and_mlp(xs, h_att, y_conv, om * h_ml, g_mix[l], w_out[l],
                           ln1_g[l], ln1_b[l], w_up[l], w_down[l], ln2_g[l], ln2_b[l])

    y_prompt = xp[:, N_META:]
    y_sample = xs
    return (y_prompt, y_sample,
            jnp.stack(kp_l), jnp.stack(vp_l), jnp.stack(cp_l), jnp.stack(Cp_l), jnp.stack(np_l), jnp.stack(mp_l),
            jnp.stack(ks_l), jnp.stack(vs_l), jnp.stack(cs_l), jnp.stack(Cs_l), jnp.stack(ns_l), jnp.stack(ms_l))
```

```python
import functools

import jax
import jax.numpy as jnp
from jax import lax
from jax.experimental import pallas as pl
from jax.experimental.pallas import tpu as pltpu

F32 = jnp.float32
BF16 = jnp.bfloat16

HEAD_DIM = 128
N_META = 16
PAGE_SIZE = 128
CONV_WIDTH = 3
LN_EPS = 1e-5
RMS_EPS = 1e-6
SB_SCALE = HEAD_DIM ** -0.5
MLSTM_K_SCALE = HEAD_DIM ** -0.5
NEG_BIG = -1e30

V7X_VMEM_LIMIT_BYTES = 56 * 1024 * 1024
LANES = 128
SUBLANES = 8
BLK = 128


def _cparams(n_axes):
    return pltpu.CompilerParams(dimension_semantics=("arbitrary",) * n_axes,
                                vmem_limit_bytes=V7X_VMEM_LIMIT_BYTES)


def _row_tile(m, cap=704):
    if m <= cap:
        return m
    best = None
    for t in range(16, cap + 1, 16):
        if m % t == 0:
            best = t
    assert best is not None, m
    return best


def _log_sigmoid_pair(z):
    sp = jnp.log1p(jnp.exp(-jnp.abs(z)))
    return jnp.minimum(z, 0.0) - sp, -jnp.maximum(z, 0.0) - sp


def _split_bf16(x):
    hi = x.astype(BF16)
    return hi, (x - hi.astype(F32)).astype(BF16)


def _proj_kernel(x_ref, w_ref, wg_ref, p_ref, g_ref, xb_ref):
    @pl.when(pl.program_id(1) == 0)
    def _():
        xb = x_ref[...].astype(BF16)
        xb_ref[...] = xb
        g_ref[...] = jnp.dot(xb, wg_ref[...], preferred_element_type=F32)

    p_ref[...] = jnp.dot(xb_ref[...], w_ref[...], preferred_element_type=F32)


def _proj(x, w, wg, tn):
    m, k = x.shape
    n = w.shape[1]
    tm = _row_tile(m)
    return pl.pallas_call(
        _proj_kernel,
        grid=(m // tm, n // tn),
        in_specs=[pl.BlockSpec((tm, k), lambda i, j: (i, 0)),
                  pl.BlockSpec((k, tn), lambda i, j: (0, j)),
                  pl.BlockSpec((k, LANES), lambda i, j: (0, 0))],
        out_specs=[pl.BlockSpec((tm, tn), lambda i, j: (i, j)),
                   pl.BlockSpec((tm, LANES), lambda i, j: (i, 0))],
        out_shape=[jax.ShapeDtypeStruct((m, n), F32),
                   jax.ShapeDtypeStruct((m, LANES), F32)],
        scratch_shapes=[pltpu.VMEM((tm, k), BF16)],
        compiler_params=_cparams(2),
        name="proj",
    )(x, w, wg)


def _sb_tile(qt, kt, vt, bias, u2, carry, acc, mask):
    s = lax.dot_general(qt, kt, (((1,), (1,)), ((), ())), preferred_element_type=F32)
    z = s * SB_SCALE + bias
    lsz, lk = _log_sigmoid_pair(z)
    if mask is not None:
        lk = jnp.where(mask, lk, 0.0)
    hi, lo = _split_bf16(lk)
    cs = jnp.dot(jnp.concatenate([hi, lo], axis=1), u2, preferred_element_type=F32)
    a = jnp.exp(lsz + cs[:, :BLK] + carry)
    if mask is not None:
        a = jnp.where(mask, a, 0.0)
    acc = acc + jnp.dot(a.astype(BF16), vt, preferred_element_type=F32)
    return carry + cs[:, BLK:], acc


def _attn_kernel(bias_ref, q_ref, k_ref, v_ref, u2_ref, o_ref, ko_ref, vo_ref, qb, kb, vb):
    t = q_ref.shape[0]
    tp = qb.shape[0]
    nblk = tp // BLK
    tail = t - (nblk - 1) * BLK
    bias = bias_ref[pl.program_id(1)]

    zpad = jnp.zeros((tp - t, HEAD_DIM), BF16)
    for src, dst in ((q_ref, qb), (k_ref, kb), (v_ref, vb)):
        dst[0:t, :] = src[...].astype(BF16)
        dst[t:tp, :] = zpad
    ko_ref[...] = k_ref[...]
    vo_ref[...] = v_ref[...]

    u2 = u2_ref[...]
    row = lax.broadcasted_iota(jnp.int32, (BLK, BLK), 0)
    col = lax.broadcasted_iota(jnp.int32, (BLK, BLK), 1)
    diag_mask = col < row
    zeros = jnp.zeros((BLK, HEAD_DIM), F32)

    def q_block(i):
        r0 = i * BLK if isinstance(i, int) else pl.multiple_of(i * BLK, BLK)
        qt = qb[pl.ds(r0, BLK), :]
        carry, acc = _sb_tile(qt, kb[pl.ds(r0, BLK), :], vb[pl.ds(r0, BLK), :], bias, u2,
                              zeros, zeros, diag_mask)

        def body(n, st):
            c0 = pl.multiple_of((i - 1 - n) * BLK, BLK)
            return _sb_tile(qt, kb[pl.ds(c0, BLK), :], vb[pl.ds(c0, BLK), :], bias, u2,
                            st[0], st[1], None)

        _, acc = lax.fori_loop(0, i, body, (carry, acc))
        return r0, acc

    def outer(i, _):
        r0, acc = q_block(i)
        o_ref[pl.ds(r0, BLK), :] = acc
        return 0

    lax.fori_loop(0, nblk - 1, outer, 0)
    _, acc = q_block(nblk - 1)
    o_ref[t - tail:t, :] = acc[:tail]


def _attn_kernel_aliased(bias_ref, q_ref, k_ref, v_ref, u2_ref, kprev_ref, vprev_ref, *rest):
    del kprev_ref, vprev_ref
    _attn_kernel(bias_ref, q_ref, k_ref, v_ref, u2_ref, *rest)


def _attn_prompt(p3, sb_bias_l, u2, kv_prev, layer, depth):
    nb, t, _ = p3.shape
    n_heads = sb_bias_l.shape[0]
    tp = pl.cdiv(t, BLK) * BLK
    seq = lambda off: pl.BlockSpec((None, t, HEAD_DIM), lambda b, h: (b, 0, off + h))
    kv_spec = pl.BlockSpec((None, None, t, HEAD_DIM), lambda b, h: (layer, b, 0, h))
    kv_shape = jax.ShapeDtypeStruct((depth, nb, t, n_heads * HEAD_DIM), F32)
    in_specs = [pl.BlockSpec(memory_space=pltpu.SMEM), seq(0), seq(n_heads), seq(2 * n_heads),
                pl.BlockSpec((2 * BLK, BLK + LANES), lambda b, h: (0, 0))]
    args = [sb_bias_l, p3, p3, p3, u2]
    aliases = {}
    body = _attn_kernel
    if kv_prev is not None:
        in_specs += [pl.BlockSpec(memory_space=pl.ANY)] * 2
        args += list(kv_prev)
        aliases = {5: 1, 6: 2}
        body = _attn_kernel_aliased
    return pl.pallas_call(
        body,
        grid=(nb, n_heads),
        in_specs=in_specs,
        out_specs=[pl.BlockSpec((None, t, HEAD_DIM), lambda b, h: (b, 0, h)), kv_spec, kv_spec],
        out_shape=[jax.ShapeDtypeStruct((nb, t, n_heads * HEAD_DIM), F32), kv_shape, kv_shape],
        scratch_shapes=[pltpu.VMEM((tp, HEAD_DIM), BF16)] * 3,
        input_output_aliases=aliases,
        compiler_params=_cparams(2),
        name="attn_prompt",
    )(*args)


def _attn_sample_kernel(pt_ref, qbd_ref, bias_ref, qi_ref, t2_ref, kn_ref, vn_ref, kc_ref, vc_ref,
                        o_ref, acc_ref, carry_ref):
    del pt_ref
    p = pl.program_id(1)
    n_cols = acc_ref.shape[0]

    @pl.when(p == 0)
    def _():
        acc_ref[...] = jnp.zeros_like(acc_ref)
        carry_ref[...] = jnp.zeros_like(carry_ref)

    def page(k, v, masked):
        z = (jnp.dot(k.astype(BF16), qbd_ref[...], preferred_element_type=F32) * SB_SCALE
             + bias_ref[...])
        lsz, lk = _log_sigmoid_pair(z)
        if masked:
            valid = lax.broadcasted_iota(jnp.int32, z.shape, 0) < qi_ref[...]
            lk = jnp.where(valid, lk, 0.0)
        hi, lo = _split_bf16(lk)
        suffix = jnp.dot(t2_ref[...], jnp.concatenate([hi, lo], axis=0),
                         preferred_element_type=F32)
        a = jnp.exp(lsz + suffix + carry_ref[...])
        if masked:
            a = jnp.where(valid, a, 0.0)
        a_t = a.T[:n_cols].astype(BF16)
        acc_ref[...] += jnp.dot(a_t, v.astype(BF16), preferred_element_type=F32)
        carry_ref[...] += jnp.sum(lk, axis=0, keepdims=True)

    @pl.when(p == 0)
    def _():
        page(kn_ref[...], vn_ref[...], True)

    @pl.when(p > 0)
    def _():
        page(kc_ref[...], vc_ref[...], False)

    @pl.when(p == pl.num_programs(1) - 1)
    def _():
        o_ref[...] = acc_ref[...]


def _attn_sample(q, k_new, v_new, cache_k4, cache_v4, page_table, sb_bias_l, layer):
    nb, n_dec, n_heads, _ = q.shape
    d_att = n_heads * HEAD_DIM
    n_pages = page_table.shape[1]
    n_cols = n_heads * n_dec
    assert n_cols <= LANES and n_cols % SUBLANES == 0
    eye = jnp.eye(n_heads, dtype=F32)
    q_t = jnp.transpose(q, (0, 2, 3, 1))
    qbd = (q_t[:, :, :, None, :] * eye[None, :, None, :, None]).reshape(nb, d_att, n_cols)
    qbd = jnp.pad(qbd, ((0, 0), (0, 0), (0, LANES - n_cols))).astype(BF16)
    bias_cols = jnp.pad(jnp.repeat(sb_bias_l, n_dec), (0, LANES - n_cols)).reshape(1, LANES)
    qi_cols = jnp.pad(jnp.tile(jnp.arange(n_dec, dtype=jnp.int32), n_heads),
                      (0, LANES - n_cols)).reshape(1, LANES)
    pad_rows = lambda a: jnp.pad(a.reshape(nb, n_dec, d_att), ((0, 0), (0, PAGE_SIZE - n_dec), (0, 0)))
    pos = jnp.arange(PAGE_SIZE)
    t_strict = (pos[None, :] > pos[:, None]).astype(BF16)
    t2 = jnp.concatenate([t_strict, t_strict], axis=1)

    def cache_map(b, p, pt):
        return (layer, pt[b, n_pages - 1 - jnp.maximum(p - 1, 0)], 0, 0)

    new_spec = pl.BlockSpec((None, PAGE_SIZE, d_att), lambda b, p, pt: (b, 0, 0))
    cache_spec = pl.BlockSpec((None, None, PAGE_SIZE, d_att), cache_map)
    lane_vec = pl.BlockSpec((1, LANES), lambda b, p, pt: (0, 0))
    out = pl.pallas_call(
        _attn_sample_kernel,
        grid_spec=pltpu.PrefetchScalarGridSpec(
            num_scalar_prefetch=1,
            grid=(nb, n_pages + 1),
            in_specs=[pl.BlockSpec((None, d_att, LANES), lambda b, p, pt: (b, 0, 0)),
                      lane_vec, lane_vec,
                      pl.BlockSpec((PAGE_SIZE, 2 * PAGE_SIZE), lambda b, p, pt: (0, 0)),
                      new_spec, new_spec, cache_spec, cache_spec],
            out_specs=pl.BlockSpec((None, n_cols, d_att), lambda b, p, pt: (b, 0, 0)),
            scratch_shapes=[pltpu.VMEM((n_cols, d_att), F32), pltpu.VMEM((1, LANES), F32)]),
        out_shape=jax.ShapeDtypeStruct((nb, n_cols, d_att), F32),
        compiler_params=_cparams(2),
        name="attn_sample",
    )(page_table, qbd, bias_cols, qi_cols, t2, pad_rows(k_new), pad_rows(v_new), cache_k4, cache_v4)
    o = out.reshape(nb, n_heads, n_dec, n_heads, HEAD_DIM)
    o = jnp.diagonal(o, axis1=1, axis2=3)
    return jnp.transpose(o, (0, 1, 3, 2)).reshape(nb, n_dec, d_att)


def _conv_kernel(bc_ref, cc_ref, xc_ref, prev_ref, w_ref, y_ref, st_ref, ext_ref):
    t = bc_ref.shape[0]
    te = ext_ref.shape[0]
    u = cc_ref[...] * xc_ref[...]
    ext_ref[0:SUBLANES, :] = prev_ref[...]
    ext_ref[te - SUBLANES:te, :] = jnp.zeros((SUBLANES, u.shape[1]), F32)
    ext_ref[SUBLANES:SUBLANES + t, :] = u
    e0 = ext_ref[...]
    e1 = pltpu.roll(e0, 1, 0)
    e2 = pltpu.roll(e0, 2, 0)
    w = w_ref[...]
    lo, hi = SUBLANES, SUBLANES + t
    mix = w[0:1] * e2[lo:hi] + w[1:2] * e1[lo:hi] + w[2:3] * e0[lo:hi]
    y_ref[...] = bc_ref[...] * mix
    st_ref[...] = ext_ref[hi - (CONV_WIDTH - 1):hi, :]


def _conv(p3, prev8, conv_w_l, col0, d_conv):
    nb, t, _ = p3.shape
    cb = col0 // d_conv
    te = SUBLANES + pl.cdiv(t, SUBLANES) * SUBLANES
    seq = lambda j: pl.BlockSpec((None, t, d_conv), lambda b: (b, 0, cb + j))
    return pl.pallas_call(
        _conv_kernel,
        grid=(nb,),
        in_specs=[seq(0), seq(1), seq(2),
                  pl.BlockSpec((None, SUBLANES, d_conv), lambda b: (b, 0, 0)),
                  pl.BlockSpec((CONV_WIDTH, d_conv), lambda b: (0, 0))],
        out_specs=[pl.BlockSpec((None, t, d_conv), lambda b: (b, 0, 0)),
                   pl.BlockSpec((None, CONV_WIDTH - 1, d_conv), lambda b: (b, 0, 0))],
        out_shape=[jax.ShapeDtypeStruct((nb, t, d_conv), F32),
                   jax.ShapeDtypeStruct((nb, CONV_WIDTH - 1, d_conv), F32)],
        scratch_shapes=[pltpu.VMEM((te, d_conv), F32)],
        compiler_params=_cparams(1),
        name="conv",
    )(p3, p3, p3, prev8, conv_w_l)


def _mlstm_kernel(bi_ref, bf_ref, q_ref, k_ref, v_ref, om_ref, g_ref, c0_ref, n0_ref, m0_ref,
                  h_ref, c_ref, n_ref, m_ref, qs, ks, vs, oms, gs):
    t = q_ref.shape[0]
    tp = qs.shape[0]
    nchunk = tp // BLK
    tail = t - (nchunk - 1) * BLK
    head = pl.program_id(1)
    b_i = bi_ref[head]
    b_f = bf_ref[head]
    n_gate_heads = bi_ref.shape[0]

    for src, dst in ((q_ref, qs), (k_ref, ks), (v_ref, vs), (om_ref, oms), (g_ref, gs)):
        dst[0:t, :] = src[...]
        if tp > t:
            dst[t:tp, :] = jnp.zeros((tp - t, dst.shape[1]), F32)

    c_ref[...] = c0_ref[...]
    n_ref[...] = n0_ref[...]
    m_ref[...] = m0_ref[...]

    row = lax.broadcasted_iota(jnp.int32, (BLK, BLK), 0)
    col = lax.broadcasted_iota(jnp.int32, (BLK, BLK), 1)
    eye = row == col
    causal = col <= row

    def chunk(ci):
        r0 = ci * BLK if isinstance(ci, int) else pl.multiple_of(ci * BLK, BLK)
        q = qs[pl.ds(r0, BLK), :].astype(BF16)
        kf = ks[pl.ds(r0, BLK), :] * MLSTM_K_SCALE
        k = kf.astype(BF16)
        vf = vs[pl.ds(r0, BLK), :]
        g = gs[pl.ds(r0, BLK), :]
        c0 = c_ref[...]
        n0 = n_ref[...]
        m0 = m_ref[:, 0:1]

        im_col = jnp.sum(jnp.where(col == head, g, 0.0), axis=1, keepdims=True)
        fm_col = jnp.sum(jnp.where(col == head + n_gate_heads, g, 0.0), axis=1, keepdims=True)
        valid = (lax.broadcasted_iota(jnp.int32, (BLK, 1), 0) + r0) < t
        ig_col = jnp.where(valid, im_col + b_i, NEG_BIG)
        lf_col = jnp.where(valid, _log_sigmoid_pair(fm_col + b_f)[0], 0.0)
        ig_row = jnp.sum(jnp.where(eye, ig_col, 0.0), axis=0, keepdims=True)
        lf_row = jnp.sum(jnp.where(eye, lf_col, 0.0), axis=0, keepdims=True)
        cf_col = jnp.sum(jnp.where(causal, lf_row, 0.0), axis=1, keepdims=True)
        cf_row = jnp.sum(jnp.where(row <= col, lf_col, 0.0), axis=0, keepdims=True)

        log_d = jnp.where(causal, cf_col - cf_row + ig_row, -jnp.inf)
        log_inter = cf_col + m0
        m_t = jnp.maximum(log_inter, jnp.max(log_d, axis=1, keepdims=True))
        qk = lax.dot_general(q, k, (((1,), (1,)), ((), ())), preferred_element_type=F32)
        w = jnp.exp(log_d - m_t) * qk
        s_inter = jnp.exp(log_inter - m_t)
        qc = lax.dot_general(q, c0.astype(BF16), (((1,), (1,)), ((), ())),
                             preferred_element_type=F32)
        num = s_inter * qc + jnp.dot(w.astype(BF16), vf.astype(BF16), preferred_element_type=F32)
        qn = jnp.sum(q.astype(F32) * n0, axis=1, keepdims=True)
        den = s_inter * qn + jnp.sum(w, axis=1, keepdims=True)
        h = num / jnp.maximum(jnp.abs(den), jnp.exp(-m_t))
        gated = jax.nn.sigmoid(oms[pl.ds(r0, BLK), :]) * h

        cf_last = jnp.sum(lf_row, axis=1, keepdims=True)
        log_w = cf_last - cf_row + ig_row
        m_new = jnp.maximum(cf_last + m0, jnp.max(log_w, axis=1, keepdims=True))
        wk_row = jnp.exp(log_w - m_new)
        decay = jnp.exp(cf_last + m0 - m_new)
        wk_col = jnp.sum(jnp.where(eye, wk_row, 0.0), axis=1, keepdims=True)
        vs_t = (vf * wk_col).T.astype(BF16)
        c_ref[...] = decay * c0 + jnp.dot(vs_t, k, preferred_element_type=F32)
        n_ref[...] = decay * n0 + jnp.sum(k.astype(F32) * wk_col, axis=0, keepdims=True)
        m_ref[...] = jnp.broadcast_to(m_new, m_ref.shape)
        return r0, gated

    def body(ci, _):
        r0, gated = chunk(ci)
        h_ref[pl.ds(r0, BLK), :] = gated
        return 0

    lax.fori_loop(0, nchunk - 1, body, 0)
    _, gated = chunk(nchunk - 1)
    h_ref[t - tail:t, :] = gated[:tail]


def _mlstm(p3, g3, b_i_l, b_f_l, c0, n0, m0, col0):
    nb, t, _ = p3.shape
    n_heads = c0.shape[1]
    cb = col0 // HEAD_DIM
    tp = pl.cdiv(t, BLK) * BLK
    seq = lambda j: pl.BlockSpec((None, t, HEAD_DIM), lambda b, h: (b, 0, cb + j * n_heads + h))
    vec_spec = pl.BlockSpec((None, None, 1, HEAD_DIM), lambda b, h: (b, h, 0, 0))
    mat_spec = pl.BlockSpec((None, None, HEAD_DIM, HEAD_DIM), lambda b, h: (b, h, 0, 0))
    smem = pl.BlockSpec(memory_space=pltpu.SMEM)
    return pl.pallas_call(
        _mlstm_kernel,
        grid=(nb, n_heads),
        in_specs=[smem, smem, seq(0), seq(1), seq(2), seq(3),
                  pl.BlockSpec((None, t, LANES), lambda b, h: (b, 0, 0)),
                  mat_spec, vec_spec, vec_spec],
        out_specs=[pl.BlockSpec((None, t, HEAD_DIM), lambda b, h: (b, 0, h)),
                   mat_spec, vec_spec, vec_spec],
        out_shape=[jax.ShapeDtypeStruct((nb, t, n_heads * HEAD_DIM), F32),
                   jax.ShapeDtypeStruct(c0.shape, F32),
                   jax.ShapeDtypeStruct(n0.shape, F32),
                   jax.ShapeDtypeStruct(m0.shape, F32)],
        scratch_shapes=[pltpu.VMEM((tp, HEAD_DIM), F32)] * 4 + [pltpu.VMEM((tp, LANES), F32)],
        compiler_params=_cparams(2),
        name="mlstm",
    )(b_i_l, b_f_l, p3, p3, p3, p3, g3, c0, n0, m0)


def _merge_kernel(ha_ref, yc_ref, hm_ref, x_ref, gmix_ref, w_ref, g_ref, b_ref, o_ref,
                  hn_ref, r_ref, *, alpha):
    j = pl.program_id(1)
    n_blk, _, tn = r_ref.shape

    @pl.when(j == 0)
    def _():
        col = 0
        for src in (ha_ref, yc_ref, hm_ref):
            for s in range(src.shape[1] // HEAD_DIM):
                seg = src[:, s * HEAD_DIM:(s + 1) * HEAD_DIM]
                ms = jnp.mean(seg * seg, axis=-1, keepdims=True)
                hn = seg * lax.rsqrt(ms + RMS_EPS) * gmix_ref[:, col:col + HEAD_DIM]
                hn_ref[:, col:col + HEAD_DIM] = hn.astype(BF16)
                col += HEAD_DIM

    r_ref[j] = alpha * x_ref[...] + jnp.dot(hn_ref[...], w_ref[...], preferred_element_type=F32)

    @pl.when(j == n_blk - 1)
    def _():
        d = n_blk * tn
        tot = sum(jnp.sum(r_ref[s], axis=-1, keepdims=True) for s in range(n_blk))
        mu = tot / d
        sq = sum(jnp.sum((r_ref[s] - mu) ** 2, axis=-1, keepdims=True) for s in range(n_blk))
        inv = lax.rsqrt(sq / d + LN_EPS)
        for s in range(n_blk):
            sl = slice(s * tn, (s + 1) * tn)
            o_ref[:, sl] = (r_ref[s] - mu) * inv * g_ref[:, sl] + b_ref[:, sl]


def _merge(h_att, y_conv, h_ml, x, g_mix_l, w_out_l, ln_g, ln_b, alpha, tn):
    m, d = x.shape
    d_mix = w_out_l.shape[0]
    tm = _row_tile(m)
    rows = lambda a: pl.BlockSpec((tm, a.shape[1]), lambda i, j: (i, 0))
    vec = lambda a: pl.BlockSpec(a.shape, lambda i, j: (0, 0))
    return pl.pallas_call(
        functools.partial(_merge_kernel, alpha=alpha),
        grid=(m // tm, d // tn),
        in_specs=[rows(h_att), rows(y_conv), rows(h_ml),
                  pl.BlockSpec((tm, tn), lambda i, j: (i, j)),
                  vec(g_mix_l),
                  pl.BlockSpec((d_mix, tn), lambda i, j: (0, j)),
                  vec(ln_g), vec(ln_b)],
        out_specs=pl.BlockSpec((tm, d), lambda i, j: (i, 0)),
        out_shape=jax.ShapeDtypeStruct((m, d), F32),
        scratch_shapes=[pltpu.VMEM((tm, d_mix), BF16), pltpu.VMEM((d // tn, tm, tn), F32)],
        compiler_params=_cparams(2),
        name="merge",
    )(h_att, y_conv, h_ml, x, g_mix_l, w_out_l, ln_g, ln_b)


def _mlp_kernel(x_ref, wu_ref, wd_ref, g_ref, b_ref, o_ref, xb_ref, acc_ref, *, alpha):
    j = pl.program_id(1)

    @pl.when(j == 0)
    def _():
        xb_ref[...] = x_ref[...].astype(BF16)

    u = jnp.maximum(jnp.dot(xb_ref[...], wu_ref[...], preferred_element_type=F32), 0.0)
    part = jnp.dot((u * u).astype(BF16), wd_ref[...], preferred_element_type=F32)

    @pl.when(j == 0)
    def _():
        acc_ref[...] = part

    @pl.when(j > 0)
    def _():
        acc_ref[...] += part

    @pl.when(j == pl.num_programs(1) - 1)
    def _():
        r = alpha * x_ref[...] + acc_ref[...]
        mu = jnp.mean(r, axis=-1, keepdims=True)
        d = r - mu
        var = jnp.mean(d * d, axis=-1, keepdims=True)
        o_ref[...] = d * lax.rsqrt(var + LN_EPS) * g_ref[...] + b_ref[...]


def _mlp(x, w_up_l, w_down_l, ln_g, ln_b, alpha, tf):
    m, d = x.shape
    d_ff = w_up_l.shape[1]
    tm = _row_tile(m)
    vec = pl.BlockSpec((1, d), lambda i, j: (0, 0))
    return pl.pallas_call(
        functools.partial(_mlp_kernel, alpha=alpha),
        grid=(m // tm, d_ff // tf),
        in_specs=[pl.BlockSpec((tm, d), lambda i, j: (i, 0)),
                  pl.BlockSpec((d, tf), lambda i, j: (0, j)),
                  pl.BlockSpec((tf, d), lambda i, j: (j, 0)),
                  vec, vec],
        out_specs=pl.BlockSpec((tm, d), lambda i, j: (i, 0)),
        out_shape=jax.ShapeDtypeStruct((m, d), F32),
        scratch_shapes=[pltpu.VMEM((tm, d), BF16), pltpu.VMEM((tm, d), F32)],
        compiler_params=_cparams(2),
        name="mlp",
    )(x, w_up_l, w_down_l, ln_g, ln_b)


def kernel(x_prompt, x_sample, cache_k, cache_v, page_table, state_conv, state_C, state_n, state_m,
           meta_tokens, w_in, sb_bias, conv_w, b_i, b_f, g_mix, w_out, ln1_g, ln1_b, w_up, w_down,
           ln2_g, ln2_b):
    depth, d_model, d_in = w_in.shape
    bp, n_seq, _ = x_prompt.shape
    bs, n_dec, _ = x_sample.shape
    n_phys, page, h_att = cache_k.shape[1:4]
    assert page == PAGE_SIZE and n_dec < PAGE_SIZE
    h_ml = b_i.shape[1]
    d_att = h_att * HEAD_DIM
    d_conv = conv_w.shape[2]
    d_ml = h_ml * HEAD_DIM
    d_main = 3 * d_att + 3 * d_conv + 4 * d_ml
    assert d_in == d_main + 2 * h_ml and 2 * h_ml <= LANES
    t_p = N_META + n_seq
    alpha = (2 * depth) ** 0.25
    col_conv = 3 * d_att
    col_ml = col_conv + 3 * d_conv
    tn_proj = d_main // 4
    tn_out = 512
    tf = 512

    w_main = w_in[:, :, :d_main].astype(BF16)
    w_gate = jnp.pad(w_in[:, :, d_main:], ((0, 0), (0, 0), (0, LANES - 2 * h_ml))).astype(BF16)
    w_out_b = w_out.astype(BF16)
    w_up_b = w_up.astype(BF16)
    w_down_b = w_down.astype(BF16)

    pos = jnp.arange(BLK)
    u_strict = (pos[:, None] > pos[None, :]).astype(BF16)
    u_half = jnp.concatenate([u_strict, jnp.ones((BLK, LANES), BF16)], axis=1)
    u2 = jnp.concatenate([u_half, u_half], axis=0)

    cache_k4 = cache_k.reshape(depth, n_phys, PAGE_SIZE, d_att)
    cache_v4 = cache_v.reshape(depth, n_phys, PAGE_SIZE, d_att)
    row2 = lambda a: a.reshape(1, -1)
    lane_bcast = lambda a: jnp.broadcast_to(a[..., None, None], a.shape + (1, LANES))

    xp = jnp.concatenate([jnp.broadcast_to(meta_tokens[None], (bp, N_META, d_model)), x_prompt],
                         axis=1).reshape(bp * t_p, d_model)
    xs = x_sample.reshape(bs * n_dec, d_model)

    zero_prev = jnp.zeros((bp, SUBLANES, d_conv), F32)
    zero_c = jnp.zeros((bp, h_ml, HEAD_DIM, HEAD_DIM), F32)
    zero_v = jnp.zeros((bp, h_ml, 1, LANES), F32)

    kv_prompt = None
    outs = {name: [] for name in ("cp", "Cp", "np", "mp", "ks", "vs", "cs", "Cs", "ns", "ms")}

    def mix_and_mlp(x, h_a, y_c, h_m, l):
        x1 = _merge(h_a, y_c, h_m, x, row2(g_mix[l]), w_out_b[l], row2(ln1_g[l]), row2(ln1_b[l]),
                    alpha, tn_out)
        return _mlp(x1, w_up_b[l], w_down_b[l], row2(ln2_g[l]), row2(ln2_b[l]), alpha, tf)

    for l in range(depth):
        p, g = _proj(xp, w_main[l], w_gate[l], tn_proj)
        p3 = p.reshape(bp, t_p, d_main)
        g3 = g.reshape(bp, t_p, LANES)
        h_a, k_all, v_all = _attn_prompt(p3, sb_bias[l], u2, kv_prompt, l, depth)
        kv_prompt = (k_all, v_all)
        y_c, conv_st = _conv(p3, zero_prev, conv_w[l], col_conv, d_conv)
        h_m, c_fin, n_fin, m_fin = _mlstm(p3, g3, b_i[l], b_f[l], zero_c, zero_v, zero_v, col_ml)
        outs["cp"].append(conv_st)
        outs["Cp"].append(c_fin)
        outs["np"].append(n_fin[:, :, 0, :])
        outs["mp"].append(m_fin[:, :, 0, 0])
        xp = mix_and_mlp(xp, h_a.reshape(bp * t_p, d_att), y_c.reshape(bp * t_p, d_conv),
                         h_m.reshape(bp * t_p, d_ml), l)

        p, g = _proj(xs, w_main[l], w_gate[l], tn_proj)
        p3 = p.reshape(bs, n_dec, d_main)
        g3 = g.reshape(bs, n_dec, LANES)
        q_s = p3[:, :, :d_att].reshape(bs, n_dec, h_att, HEAD_DIM)
        k_s = p3[:, :, d_att:2 * d_att].reshape(bs, n_dec, h_att, HEAD_DIM)
        v_s = p3[:, :, 2 * d_att:3 * d_att].reshape(bs, n_dec, h_att, HEAD_DIM)
        h_a = _attn_sample(q_s, k_s, v_s, cache_k4, cache_v4, page_table, sb_bias[l], l)
        prev8 = jnp.pad(state_conv[l], ((0, 0), (SUBLANES - (CONV_WIDTH - 1), 0), (0, 0)))
        y_c, conv_st = _conv(p3, prev8, conv_w[l], col_conv, d_conv)
        h_m, c_fin, n_fin, m_fin = _mlstm(p3, g3, b_i[l], b_f[l], state_C[l],
                                          state_n[l][:, :, None, :], lane_bcast(state_m[l]), col_ml)
        outs["ks"].append(k_s)
        outs["vs"].append(v_s)
        outs["cs"].append(conv_st)
        outs["Cs"].append(c_fin)
        outs["ns"].append(n_fin[:, :, 0, :])
        outs["ms"].append(m_fin[:, :, 0, 0])
        xs = mix_and_mlp(xs, h_a.reshape(bs * n_dec, d_att), y_c.reshape(bs * n_dec, d_conv),
                         h_m.reshape(bs * n_dec, d_ml), l)

    y_prompt = xp.reshape(bp, t_p, d_model)[:, N_META:]
    y_sample = xs.reshape(bs, n_dec, d_model)
    k_prompt = kv_prompt[0].reshape(depth, bp, t_p, h_att, HEAD_DIM)
    v_prompt = kv_prompt[1].reshape(depth, bp, t_p, h_att, HEAD_DIM)
    st = lambda name: jnp.stack(outs[name])
    return (y_prompt, y_sample, k_prompt, v_prompt, st("cp"), st("Cp"), st("np"), st("mp"),
            st("ks"), st("vs"), st("cs"), st("Cs"), st("ns"), st("ms"))
```

```python
import functools

import jax
import jax.numpy as jnp
from jax import lax
from jax.experimental import pallas as pl
from jax.experimental.pallas import tpu as pltpu

F32 = jnp.float32
BF16 = jnp.bfloat16

HEAD_DIM = 128
N_META = 16
PAGE_SIZE = 128
CONV_WIDTH = 3
LN_EPS = 1e-5
RMS_EPS = 1e-6
SB_SCALE = HEAD_DIM ** -0.5
MLSTM_K_SCALE = HEAD_DIM ** -0.5
NEG_BIG = -1e30

V7X_VMEM_LIMIT_BYTES = 56 * 1024 * 1024
LANES = 128
SUBLANES = 8
BLK = 128


def _cparams(n_axes):
    return pltpu.CompilerParams(dimension_semantics=("arbitrary",) * n_axes,
                                vmem_limit_bytes=V7X_VMEM_LIMIT_BYTES)


def _row_tile(m, cap=704):
    if m <= cap:
        return m
    best = None
    for t in range(16, cap + 1, 16):
        if m % t == 0:
            best = t
    assert best is not None, m
    return best


def _log_sigmoid_pair(z):
    sp = jnp.log(1.0 + jnp.exp(-jnp.abs(z)))
    return jnp.minimum(z, 0.0) - sp, -jnp.maximum(z, 0.0) - sp


def _split_bf16(x):
    hi = x.astype(BF16)
    return hi, (x - hi.astype(F32)).astype(BF16)


def _proj_kernel(x_ref, w_ref, wg_ref, p_ref, g_ref, xb_ref):
    @pl.when(pl.program_id(1) == 0)
    def _():
        xb = x_ref[...].astype(BF16)
        xb_ref[...] = xb
        g_ref[...] = jnp.dot(xb, wg_ref[...], preferred_element_type=F32)

    p_ref[...] = jnp.dot(xb_ref[...], w_ref[...], preferred_element_type=F32)


def _proj(x, w, wg, tn):
    m, k = x.shape
    n = w.shape[1]
    tm = _row_tile(m)
    return pl.pallas_call(
        _proj_kernel,
        grid=(m // tm, n // tn),
        in_specs=[pl.BlockSpec((tm, k), lambda i, j: (i, 0)),
                  pl.BlockSpec((k, tn), lambda i, j: (0, j)),
                  pl.BlockSpec((k, LANES), lambda i, j: (0, 0))],
        out_specs=[pl.BlockSpec((tm, tn), lambda i, j: (i, j)),
                   pl.BlockSpec((tm, LANES), lambda i, j: (i, 0))],
        out_shape=[jax.ShapeDtypeStruct((m, n), F32),
                   jax.ShapeDtypeStruct((m, LANES), F32)],
        scratch_shapes=[pltpu.VMEM((tm, k), BF16)],
        compiler_params=_cparams(2),
        name="proj",
    )(x, w, wg)


def _qk(qt, kt):
    return lax.dot_general(qt, kt, (((1,), (1,)), ((), ())), preferred_element_type=F32)


def _sb_scores(s, bias, u2, mask):
    tq, nk = s.shape
    m = nk // BLK
    lsz, lk = _log_sigmoid_pair(s * SB_SCALE + bias)
    if mask is not None:
        lk = jnp.where(mask, lk, 0.0)
    hi, lo = _split_bf16(lk)
    lanes = lambda x, j: x[:, j * BLK:(j + 1) * BLK]
    stacked = jnp.concatenate(
        [jnp.concatenate([lanes(hi, j), lanes(lo, j)], axis=1) for j in range(m)], axis=0)
    cs = jnp.dot(stacked, u2, preferred_element_type=F32)
    pieces, later = [None] * m, None
    for j in range(m - 1, -1, -1):
        sfx, tot = cs[j * tq:(j + 1) * tq, :BLK], cs[j * tq:(j + 1) * tq, BLK:]
        pieces[j] = lanes(lsz, j) + (sfx if later is None else sfx + later)
        later = tot if later is None else later + tot
    return (pieces[0] if m == 1 else jnp.concatenate(pieces, axis=1)), later


def _sb_weighted(logit, carry, vt, mask):
    if carry is not None:
        m = logit.shape[1] // BLK
        logit = logit + (carry if m == 1 else jnp.concatenate([carry] * m, axis=1))
    a = jnp.exp(logit)
    if mask is not None:
        a = jnp.where(mask, a, 0.0)
    return jnp.dot(a.astype(BF16), vt, preferred_element_type=F32)


def _attn_kernel(bias_ref, q_ref, k_ref, v_ref, u2_ref, o_ref, ko_ref, vo_ref,
                 qb, kb, vb, carry_ref, acc_ref, *s_refs, n_lead, run):
    t = q_ref.shape[0]
    tq = carry_ref.shape[0]
    n_run = tq // run
    n_qblk = (t - n_lead) // tq
    bias = bias_ref[pl.program_id(1)]

    qb[...] = q_ref[...].astype(BF16)
    kb[...] = k_ref[...].astype(BF16)
    vb[...] = v_ref[...].astype(BF16)
    ko_ref[...] = k_ref[...]
    vo_ref[...] = v_ref[...]
    u2 = u2_ref[...]

    row = lax.broadcasted_iota(jnp.int32, (BLK, BLK), 0)
    col = lax.broadcasted_iota(jnp.int32, (BLK, BLK), 1)
    logit, _ = _sb_scores(_qk(qb[0:BLK, :], kb[0:BLK, :]), bias, u2, col < row)
    o_ref[0:n_lead, :] = _sb_weighted(logit, None, vb[0:BLK, :], col < row)[0:n_lead]

    qrow = lax.broadcasted_iota(jnp.int32, (tq, run), 0)
    qcol = lax.broadcasted_iota(jnp.int32, (tq, run), 1)

    def q_block(i, _):
        r0 = pl.multiple_of(n_lead + i * tq, 16)
        qt = qb[pl.ds(r0, tq), :]

        def products(c0):
            return _qk(qt, kb[pl.ds(pl.multiple_of(jnp.maximum(c0, n_lead), 16), run), :])

        def step(s, c0):
            logit, tot = _sb_scores(s, bias, u2, None)
            carry = carry_ref[...]
            acc_ref[...] += _sb_weighted(logit, carry, vb[pl.ds(pl.multiple_of(c0, 16), run), :], None)
            carry_ref[...] = carry + tot

        carry = acc = None
        for d in range(n_run - 1, -1, -1):
            c0 = pl.multiple_of(r0 + d * run, 16)
            mask = qcol + d * run < qrow
            logit, tot = _sb_scores(_qk(qt, kb[pl.ds(c0, run), :]), bias, u2, mask)
            pv = _sb_weighted(logit, carry, vb[pl.ds(c0, run), :], mask)
            acc = pv if acc is None else acc + pv
            carry = tot if carry is None else carry + tot
        carry_ref[...] = carry
        acc_ref[...] = acc

        if n_run % 2 == 0:
            s_even, s_odd = s_refs
            s_even[...] = products(r0 - run)

            def pair(n, _):
                c0 = r0 - (2 * n + 1) * run
                s = s_even[...]
                s_odd[...] = products(c0 - run)
                step(s, c0)
                s = s_odd[...]
                s_even[...] = products(c0 - 2 * run)
                step(s, c0 - run)
                return 0

            lax.fori_loop(0, i * (n_run // 2), pair, 0)
        else:
            def single(n, _):
                c0 = r0 - (n + 1) * run
                step(products(c0), c0)
                return 0

            lax.fori_loop(0, i * n_run, single, 0)

        mask = lax.broadcasted_iota(jnp.int32, (tq, BLK), 1) < n_lead
        logit, _ = _sb_scores(_qk(qt, kb[0:BLK, :]), bias, u2, mask)
        o_ref[pl.ds(r0, tq), :] = acc_ref[...] + _sb_weighted(logit, carry_ref[...], vb[0:BLK, :], mask)
        return 0

    lax.fori_loop(0, n_qblk, q_block, 0)


def _attn_kernel_aliased(bias_ref, q_ref, k_ref, v_ref, u2_ref, kprev_ref, vprev_ref, *rest, **kw):
    del kprev_ref, vprev_ref
    _attn_kernel(bias_ref, q_ref, k_ref, v_ref, u2_ref, *rest, **kw)


def _attn_prompt(p3, sb_bias_l, u2, kv_prev, layer, depth, n_lead):
    nb, t, _ = p3.shape
    n_heads = sb_bias_l.shape[0]
    assert 0 < n_lead <= BLK <= t and n_lead % 16 == 0
    tq = max(c for c in (512, 256, 128) if (t - n_lead) % c == 0)
    run = min(tq, 256)
    seq = lambda off: pl.BlockSpec((None, t, HEAD_DIM), lambda b, h: (b, 0, off + h))
    kv_spec = pl.BlockSpec((None, None, t, HEAD_DIM), lambda b, h: (layer, b, 0, h))
    kv_shape = jax.ShapeDtypeStruct((depth, nb, t, n_heads * HEAD_DIM), F32)
    in_specs = [pl.BlockSpec(memory_space=pltpu.SMEM), seq(0), seq(n_heads), seq(2 * n_heads),
                pl.BlockSpec((2 * BLK, BLK + LANES), lambda b, h: (0, 0))]
    args = [sb_bias_l, p3, p3, p3, u2]
    aliases = {}
    body = _attn_kernel
    if kv_prev is not None:
        in_specs += [pl.BlockSpec(memory_space=pl.ANY)] * 2
        args += list(kv_prev)
        aliases = {5: 1, 6: 2}
        body = _attn_kernel_aliased
    return pl.pallas_call(
        functools.partial(body, n_lead=n_lead, run=run),
        grid=(nb, n_heads),
        in_specs=in_specs,
        out_specs=[pl.BlockSpec((None, t, HEAD_DIM), lambda b, h: (b, 0, h)), kv_spec, kv_spec],
        out_shape=[jax.ShapeDtypeStruct((nb, t, n_heads * HEAD_DIM), F32), kv_shape, kv_shape],
        scratch_shapes=[pltpu.VMEM((t, HEAD_DIM), BF16)] * 3 + [pltpu.VMEM((tq, LANES), F32)] * 2
                       + [pltpu.VMEM((tq, run), F32)] * 2,
        input_output_aliases=aliases,
        compiler_params=_cparams(2),
        name="attn_prompt",
    )(*args)


def _attn_sample_kernel(pt_ref, q_ref, bias_ref, own_ref, new_ref, u2_ref, kn_ref, vn_ref, *refs,
                        n_group):
    del pt_ref
    kc_refs, vc_refs = refs[:n_group], refs[n_group:2 * n_group]
    o_ref, acc_ref, carry_ref = refs[2 * n_group:]
    p = pl.program_id(1)
    n_rows = acc_ref.shape[0]
    n_lanes = own_ref.shape[1]
    n_blk = n_lanes // LANES

    def scores(k_ref, keep):
        k2 = k_ref[...].reshape(n_lanes, HEAD_DIM).astype(BF16)
        s = lax.dot_general(q_ref[...], k2, (((1,), (1,)), ((), ())), preferred_element_type=F32)
        lsz, lk = _log_sigmoid_pair(s * SB_SCALE + bias_ref[...])
        hi, lo = _split_bf16(lk * keep)
        lane_blocks = lambda x: jnp.concatenate(
            [x[:, j * LANES:(j + 1) * LANES] for j in range(n_blk)], axis=0)
        return lsz, jnp.concatenate([lane_blocks(hi), lane_blocks(lo)], axis=1)

    def weighted(lsz, cs, v_ref, keep, acc, carry):
        logits = [None] * n_blk
        for j in range(n_blk - 1, -1, -1):
            rows = slice(j * n_rows, (j + 1) * n_rows)
            logits[j] = cs[rows, :LANES] + carry
            carry = carry + cs[rows, LANES:]
        a = jnp.exp(lsz + jnp.concatenate(logits, axis=1)) * keep
        v2 = v_ref[...].reshape(n_lanes, HEAD_DIM).astype(BF16)
        return acc + jnp.dot(a.astype(BF16), v2, preferred_element_type=F32), carry

    def pages(k_refs, v_refs, keep, acc, carry):
        parts = [scores(k_ref, keep) for k_ref in k_refs]
        cs = jnp.dot(jnp.concatenate([x for _, x in parts], axis=0), u2_ref[...],
                     preferred_element_type=F32)
        per_page = n_blk * n_rows
        for g, v_ref in enumerate(v_refs):
            acc, carry = weighted(parts[g][0], cs[g * per_page:(g + 1) * per_page], v_ref, keep,
                                  acc, carry)
        return acc, carry

    @pl.when(p == 0)
    def _():
        acc, carry = pages([kn_ref], [vn_ref], new_ref[...], jnp.zeros(acc_ref.shape, F32),
                           jnp.zeros(carry_ref.shape, F32))
        acc_ref[...] = acc
        carry_ref[...] = carry

    @pl.when(p > 0)
    def _():
        acc, carry = pages(kc_refs, vc_refs, own_ref[...], acc_ref[...], carry_ref[...])
        acc_ref[...] = acc
        carry_ref[...] = carry

    @pl.when(p == pl.num_programs(1) - 1)
    def _():
        o_ref[...] = acc_ref[...]


def _attn_sample(q, k_new, v_new, cache_k, cache_v, page_table, sb_bias_l, u2, layer):
    nb, n_dec, n_heads, _ = q.shape
    n_pages = page_table.shape[1]
    n_rows = n_heads * n_dec
    n_lanes = PAGE_SIZE * n_heads
    assert n_rows % 16 == 0 and n_heads == SUBLANES
    n_group = max(c for c in (8, 4, 2, 1) if n_pages % c == 0)
    q_rows = jnp.transpose(q, (0, 2, 1, 3)).reshape(nb, n_rows, HEAD_DIM).astype(BF16)
    row_head = jnp.repeat(jnp.arange(n_heads), n_dec)[:, None]
    row_query = jnp.tile(jnp.arange(n_dec), n_heads)[:, None]
    lane_pos = jnp.repeat(jnp.arange(PAGE_SIZE), n_heads)[None, :]
    lane_head = jnp.tile(jnp.arange(n_heads), PAGE_SIZE)[None, :]
    own = (row_head == lane_head).astype(F32)
    own_new = own * (lane_pos < row_query).astype(F32)
    bias_rows = jnp.broadcast_to(jnp.repeat(sb_bias_l, n_dec)[:, None], (n_rows, n_lanes))
    pad_pos = lambda a: jnp.pad(a, ((0, 0), (0, PAGE_SIZE - n_dec), (0, 0), (0, 0)))

    def cache_spec(g):
        def index_map(b, p, pt):
            logical = n_pages - 1 - (jnp.maximum(p - 1, 0) * n_group + g)
            return (layer, pt[b, logical], 0, 0, 0)
        return pl.BlockSpec((None, None, PAGE_SIZE, n_heads, HEAD_DIM), index_map)

    new_spec = pl.BlockSpec((None, PAGE_SIZE, n_heads, HEAD_DIM), lambda b, p, pt: (b, 0, 0, 0))
    row_const = pl.BlockSpec((n_rows, n_lanes), lambda b, p, pt: (0, 0))
    cache_specs = [cache_spec(g) for g in range(n_group)]
    out = pl.pallas_call(
        functools.partial(_attn_sample_kernel, n_group=n_group),
        grid_spec=pltpu.PrefetchScalarGridSpec(
            num_scalar_prefetch=1,
            grid=(nb, n_pages // n_group + 1),
            in_specs=[pl.BlockSpec((None, n_rows, HEAD_DIM), lambda b, p, pt: (b, 0, 0)),
                      row_const, row_const, row_const,
                      pl.BlockSpec(u2.shape, lambda b, p, pt: (0, 0)),
                      new_spec, new_spec] + cache_specs + cache_specs,
            out_specs=pl.BlockSpec((None, n_rows, HEAD_DIM), lambda b, p, pt: (b, 0, 0)),
            scratch_shapes=[pltpu.VMEM((n_rows, HEAD_DIM), F32), pltpu.VMEM((n_rows, LANES), F32)]),
        out_shape=jax.ShapeDtypeStruct((nb, n_rows, HEAD_DIM), F32),
        compiler_params=_cparams(2),
        name="attn_sample",
    )(page_table, q_rows, bias_rows, own, own_new, u2, pad_pos(k_new), pad_pos(v_new),
      *([cache_k] * n_group), *([cache_v] * n_group))
    return jnp.transpose(out.reshape(nb, n_heads, n_dec, HEAD_DIM), (0, 2, 1, 3)).reshape(
        nb, n_dec, n_heads * HEAD_DIM)


def _conv_kernel(bc_ref, cc_ref, xc_ref, prev_ref, w_ref, y_ref, st_ref, ext_ref):
    t = bc_ref.shape[0]
    te = ext_ref.shape[0]
    u = cc_ref[...] * xc_ref[...]
    ext_ref[0:SUBLANES, :] = prev_ref[...]
    ext_ref[te - SUBLANES:te, :] = jnp.zeros((SUBLANES, u.shape[1]), F32)
    ext_ref[SUBLANES:SUBLANES + t, :] = u
    e0 = ext_ref[...]
    e1 = pltpu.roll(e0, 1, 0)
    e2 = pltpu.roll(e0, 2, 0)
    w = w_ref[...]
    lo, hi = SUBLANES, SUBLANES + t
    mix = w[0:1] * e2[lo:hi] + w[1:2] * e1[lo:hi] + w[2:3] * e0[lo:hi]
    y_ref[...] = bc_ref[...] * mix
    st_ref[...] = ext_ref[hi - (CONV_WIDTH - 1):hi, :]


def _conv(p3, prev8, conv_w_l, col0, d_conv):
    nb, t, _ = p3.shape
    cb = col0 // d_conv
    te = SUBLANES + pl.cdiv(t, SUBLANES) * SUBLANES
    seq = lambda j: pl.BlockSpec((None, t, d_conv), lambda b: (b, 0, cb + j))
    return pl.pallas_call(
        _conv_kernel,
        grid=(nb,),
        in_specs=[seq(0), seq(1), seq(2),
                  pl.BlockSpec((None, SUBLANES, d_conv), lambda b: (b, 0, 0)),
                  pl.BlockSpec((CONV_WIDTH, d_conv), lambda b: (0, 0))],
        out_specs=[pl.BlockSpec((None, t, d_conv), lambda b: (b, 0, 0)),
                   pl.BlockSpec((None, CONV_WIDTH - 1, d_conv), lambda b: (b, 0, 0))],
        out_shape=[jax.ShapeDtypeStruct((nb, t, d_conv), F32),
                   jax.ShapeDtypeStruct((nb, CONV_WIDTH - 1, d_conv), F32)],
        scratch_shapes=[pltpu.VMEM((te, d_conv), F32)],
        compiler_params=_cparams(1),
        name="conv",
    )(p3, p3, p3, prev8, conv_w_l)


def _mlstm_kernel(bi_ref, bf_ref, q_ref, k_ref, v_ref, om_ref, g_ref, c0_ref, n0_ref, m0_ref,
                  h_ref, c_ref, n_ref, m_ref, qs, ks, vs, oms, gs):
    t = q_ref.shape[0]
    tp = qs.shape[0]
    nchunk = tp // BLK
    tail = t - (nchunk - 1) * BLK
    head = pl.program_id(1)
    b_i = bi_ref[head]
    b_f = bf_ref[head]
    n_gate_heads = bi_ref.shape[0]

    for src, dst in ((q_ref, qs), (k_ref, ks), (v_ref, vs), (om_ref, oms), (g_ref, gs)):
        dst[0:t, :] = src[...]
        if tp > t:
            dst[t:tp, :] = jnp.zeros((tp - t, dst.shape[1]), F32)

    c_ref[...] = c0_ref[...]
    n_ref[...] = n0_ref[...]
    m_ref[...] = m0_ref[...]

    row = lax.broadcasted_iota(jnp.int32, (BLK, BLK), 0)
    col = lax.broadcasted_iota(jnp.int32, (BLK, BLK), 1)
    eye = row == col
    causal = col <= row

    def chunk(ci):
        r0 = ci * BLK if isinstance(ci, int) else pl.multiple_of(ci * BLK, BLK)
        q = qs[pl.ds(r0, BLK), :].astype(BF16)
        kf = ks[pl.ds(r0, BLK), :] * MLSTM_K_SCALE
        k = kf.astype(BF16)
        vf = vs[pl.ds(r0, BLK), :]
        g = gs[pl.ds(r0, BLK), :]
        c0 = c_ref[...]
        n0 = n_ref[...]
        m0 = m_ref[:, 0:1]

        im_col = jnp.sum(jnp.where(col == head, g, 0.0), axis=1, keepdims=True)
        fm_col = jnp.sum(jnp.where(col == head + n_gate_heads, g, 0.0), axis=1, keepdims=True)
        valid = (lax.broadcasted_iota(jnp.int32, (BLK, 1), 0) + r0) < t
        ig_col = jnp.where(valid, im_col + b_i, NEG_BIG)
        lf_col = jnp.where(valid, _log_sigmoid_pair(fm_col + b_f)[0], 0.0)
        ig_row = jnp.sum(jnp.where(eye, ig_col, 0.0), axis=0, keepdims=True)
        lf_row = jnp.sum(jnp.where(eye, lf_col, 0.0), axis=0, keepdims=True)
        cf_col = jnp.sum(jnp.where(causal, lf_row, 0.0), axis=1, keepdims=True)
        cf_row = jnp.sum(jnp.where(row <= col, lf_col, 0.0), axis=0, keepdims=True)

        log_d = jnp.where(causal, cf_col - cf_row + ig_row, -jnp.inf)
        log_inter = cf_col + m0
        m_t = jnp.maximum(log_inter, jnp.max(log_d, axis=1, keepdims=True))
        qk = lax.dot_general(q, k, (((1,), (1,)), ((), ())), preferred_element_type=F32)
        w = jnp.exp(log_d - m_t) * qk
        s_inter = jnp.exp(log_inter - m_t)
        qc = lax.dot_general(q, c0.astype(BF16), (((1,), (1,)), ((), ())),
                             preferred_element_type=F32)
        num = s_inter * qc + jnp.dot(w.astype(BF16), vf.astype(BF16), preferred_element_type=F32)
        qn = jnp.sum(q.astype(F32) * n0, axis=1, keepdims=True)
        den = s_inter * qn + jnp.sum(w, axis=1, keepdims=True)
        h = num / jnp.maximum(jnp.abs(den), jnp.exp(-m_t))
        gated = jax.nn.sigmoid(oms[pl.ds(r0, BLK), :]) * h

        cf_last = jnp.sum(lf_row, axis=1, keepdims=True)
        log_w = cf_last - cf_row + ig_row
        m_new = jnp.maximum(cf_last + m0, jnp.max(log_w, axis=1, keepdims=True))
        wk_row = jnp.exp(log_w - m_new)
        decay = jnp.exp(cf_last + m0 - m_new)
        wk_col = jnp.sum(jnp.where(eye, wk_row, 0.0), axis=1, keepdims=True)
        vs_t = (vf * wk_col).T.astype(BF16)
        c_ref[...] = decay * c0 + jnp.dot(vs_t, k, preferred_element_type=F32)
        n_ref[...] = decay * n0 + jnp.sum(k.astype(F32) * wk_col, axis=0, keepdims=True)
        m_ref[...] = jnp.broadcast_to(m_new, m_ref.shape)
        return r0, gated

    def body(ci, _):
        r0, gated = chunk(ci)
        h_ref[pl.ds(r0, BLK), :] = gated
        return 0

    lax.fori_loop(0, nchunk - 1, body, 0)
    _, gated = chunk(nchunk - 1)
    h_ref[t - tail:t, :] = gated[:tail]


def _mlstm(p3, g3, b_i_l, b_f_l, c0, n0, m0, col0):
    nb, t, _ = p3.shape
    n_heads = c0.shape[1]
    cb = col0 // HEAD_DIM
    tp = pl.cdiv(t, BLK) * BLK
    seq = lambda j: pl.BlockSpec((None, t, HEAD_DIM), lambda b, h: (b, 0, cb + j * n_heads + h))
    vec_spec = pl.BlockSpec((None, None, 1, HEAD_DIM), lambda b, h: (b, h, 0, 0))
    mat_spec = pl.BlockSpec((None, None, HEAD_DIM, HEAD_DIM), lambda b, h: (b, h, 0, 0))
    smem = pl.BlockSpec(memory_space=pltpu.SMEM)
    return pl.pallas_call(
        _mlstm_kernel,
        grid=(nb, n_heads),
        in_specs=[smem, smem, seq(0), seq(1), seq(2), seq(3),
                  pl.BlockSpec((None, t, LANES), lambda b, h: (b, 0, 0)),
                  mat_spec, vec_spec, vec_spec],
        out_specs=[pl.BlockSpec((None, t, HEAD_DIM), lambda b, h: (b, 0, h)),
                   mat_spec, vec_spec, vec_spec],
        out_shape=[jax.ShapeDtypeStruct((nb, t, n_heads * HEAD_DIM), F32),
                   jax.ShapeDtypeStruct(c0.shape, F32),
                   jax.ShapeDtypeStruct(n0.shape, F32),
                   jax.ShapeDtypeStruct(m0.shape, F32)],
        scratch_shapes=[pltpu.VMEM((tp, HEAD_DIM), F32)] * 4 + [pltpu.VMEM((tp, LANES), F32)],
        compiler_params=_cparams(2),
        name="mlstm",
    )(b_i_l, b_f_l, p3, p3, p3, p3, g3, c0, n0, m0)


def _merge_kernel(ha_ref, yc_ref, hm_ref, x_ref, gmix_ref, w_ref, g_ref, b_ref, o_ref,
                  hn_ref, r_ref, *, alpha):
    j = pl.program_id(1)
    n_blk, _, tn = r_ref.shape

    @pl.when(j == 0)
    def _():
        col = 0
        for src in (ha_ref, yc_ref, hm_ref):
            for s in range(src.shape[1] // HEAD_DIM):
                seg = src[:, s * HEAD_DIM:(s + 1) * HEAD_DIM]
                ms = jnp.mean(seg * seg, axis=-1, keepdims=True)
                hn = seg * lax.rsqrt(ms + RMS_EPS) * gmix_ref[:, col:col + HEAD_DIM]
                hn_ref[:, col:col + HEAD_DIM] = hn.astype(BF16)
                col += HEAD_DIM

    r_ref[j] = alpha * x_ref[...] + jnp.dot(hn_ref[...], w_ref[...], preferred_element_type=F32)

    @pl.when(j == n_blk - 1)
    def _():
        d = n_blk * tn
        tot = sum(jnp.sum(r_ref[s], axis=-1, keepdims=True) for s in range(n_blk))
        mu = tot / d
        sq = sum(jnp.sum((r_ref[s] - mu) ** 2, axis=-1, keepdims=True) for s in range(n_blk))
        inv = lax.rsqrt(sq / d + LN_EPS)
        for s in range(n_blk):
            sl = slice(s * tn, (s + 1) * tn)
            o_ref[:, sl] = (r_ref[s] - mu) * inv * g_ref[:, sl] + b_ref[:, sl]


def _merge(h_att, y_conv, h_ml, x, g_mix_l, w_out_l, ln_g, ln_b, alpha, tn):
    m, d = x.shape
    d_mix = w_out_l.shape[0]
    tm = _row_tile(m)
    rows = lambda a: pl.BlockSpec((tm, a.shape[1]), lambda i, j: (i, 0))
    vec = lambda a: pl.BlockSpec(a.shape, lambda i, j: (0, 0))
    return pl.pallas_call(
        functools.partial(_merge_kernel, alpha=alpha),
        grid=(m // tm, d // tn),
        in_specs=[rows(h_att), rows(y_conv), rows(h_ml),
                  pl.BlockSpec((tm, tn), lambda i, j: (i, j)),
                  vec(g_mix_l),
                  pl.BlockSpec((d_mix, tn), lambda i, j: (0, j)),
                  vec(ln_g), vec(ln_b)],
        out_specs=pl.BlockSpec((tm, d), lambda i, j: (i, 0)),
        out_shape=jax.ShapeDtypeStruct((m, d), F32),
        scratch_shapes=[pltpu.VMEM((tm, d_mix), BF16), pltpu.VMEM((d // tn, tm, tn), F32)],
        compiler_params=_cparams(2),
        name="merge",
    )(h_att, y_conv, h_ml, x, g_mix_l, w_out_l, ln_g, ln_b)


def _mlp_kernel(x_ref, wu_ref, wd_ref, g_ref, b_ref, o_ref, xb_ref, acc_ref, *, alpha):
    j = pl.program_id(1)

    @pl.when(j == 0)
    def _():
        xb_ref[...] = x_ref[...].astype(BF16)

    u = jnp.maximum(jnp.dot(xb_ref[...], wu_ref[...], preferred_element_type=F32), 0.0)
    part = jnp.dot((u * u).astype(BF16), wd_ref[...], preferred_element_type=F32)

    @pl.when(j == 0)
    def _():
        acc_ref[...] = part

    @pl.when(j > 0)
    def _():
        acc_ref[...] += part

    @pl.when(j == pl.num_programs(1) - 1)
    def _():
        r = alpha * x_ref[...] + acc_ref[...]
        mu = jnp.mean(r, axis=-1, keepdims=True)
        d = r - mu
        var = jnp.mean(d * d, axis=-1, keepdims=True)
        o_ref[...] = d * lax.rsqrt(var + LN_EPS) * g_ref[...] + b_ref[...]


def _mlp(x, w_up_l, w_down_l, ln_g, ln_b, alpha, tf):
    m, d = x.shape
    d_ff = w_up_l.shape[1]
    tm = _row_tile(m)
    vec = pl.BlockSpec((1, d), lambda i, j: (0, 0))
    return pl.pallas_call(
        functools.partial(_mlp_kernel, alpha=alpha),
        grid=(m // tm, d_ff // tf),
        in_specs=[pl.BlockSpec((tm, d), lambda i, j: (i, 0)),
                  pl.BlockSpec((d, tf), lambda i, j: (0, j)),
                  pl.BlockSpec((tf, d), lambda i, j: (j, 0)),
                  vec, vec],
        out_specs=pl.BlockSpec((tm, d), lambda i, j: (i, 0)),
        out_shape=jax.ShapeDtypeStruct((m, d), F32),
        scratch_shapes=[pltpu.VMEM((tm, d), BF16), pltpu.VMEM((tm, d), F32)],
        compiler_params=_cparams(2),
        name="mlp",
    )(x, w_up_l, w_down_l, ln_g, ln_b)


def kernel(x_prompt, x_sample, cache_k, cache_v, page_table, state_conv, state_C, state_n, state_m,
           meta_tokens, w_in, sb_bias, conv_w, b_i, b_f, g_mix, w_out, ln1_g, ln1_b, w_up, w_down,
           ln2_g, ln2_b):
    depth, d_model, d_in = w_in.shape
    bp, n_seq, _ = x_prompt.shape
    bs, n_dec, _ = x_sample.shape
    n_phys, page, h_att = cache_k.shape[1:4]
    assert page == PAGE_SIZE and n_dec < PAGE_SIZE
    h_ml = b_i.shape[1]
    d_att = h_att * HEAD_DIM
    d_conv = conv_w.shape[2]
    d_ml = h_ml * HEAD_DIM
    d_main = 3 * d_att + 3 * d_conv + 4 * d_ml
    assert d_in == d_main + 2 * h_ml and 2 * h_ml <= LANES
    t_p = N_META + n_seq
    alpha = (2 * depth) ** 0.25
    col_conv = 3 * d_att
    col_ml = col_conv + 3 * d_conv
    tn_proj = d_main // 4
    tn_out = 512
    tf = 512

    w_main = w_in[:, :, :d_main].astype(BF16)
    w_gate = jnp.pad(w_in[:, :, d_main:], ((0, 0), (0, 0), (0, LANES - 2 * h_ml))).astype(BF16)
    w_out_b = w_out.astype(BF16)
    w_up_b = w_up.astype(BF16)
    w_down_b = w_down.astype(BF16)

    pos = jnp.arange(BLK)
    u_strict = (pos[:, None] > pos[None, :]).astype(BF16)
    u_half = jnp.concatenate([u_strict, jnp.ones((BLK, LANES), BF16)], axis=1)
    u2 = jnp.concatenate([u_half, u_half], axis=0)

    row2 = lambda a: a.reshape(1, -1)
    lane_bcast = lambda a: jnp.broadcast_to(a[..., None, None], a.shape + (1, LANES))

    xp = jnp.concatenate([jnp.broadcast_to(meta_tokens[None], (bp, N_META, d_model)), x_prompt],
                         axis=1).reshape(bp * t_p, d_model)
    xs = x_sample.reshape(bs * n_dec, d_model)

    zero_prev = jnp.zeros((bp, SUBLANES, d_conv), F32)
    zero_c = jnp.zeros((bp, h_ml, HEAD_DIM, HEAD_DIM), F32)
    zero_v = jnp.zeros((bp, h_ml, 1, LANES), F32)

    kv_prompt = None
    outs = {name: [] for name in ("cp", "Cp", "np", "mp", "ks", "vs", "cs", "Cs", "ns", "ms")}

    def mix_and_mlp(x, h_a, y_c, h_m, l):
        x1 = _merge(h_a, y_c, h_m, x, row2(g_mix[l]), w_out_b[l], row2(ln1_g[l]), row2(ln1_b[l]),
                    alpha, tn_out)
        return _mlp(x1, w_up_b[l], w_down_b[l], row2(ln2_g[l]), row2(ln2_b[l]), alpha, tf)

    for l in range(depth):
        p, g = _proj(xp, w_main[l], w_gate[l], tn_proj)
        p3 = p.reshape(bp, t_p, d_main)
        g3 = g.reshape(bp, t_p, LANES)
        h_a, k_all, v_all = _attn_prompt(p3, sb_bias[l], u2, kv_prompt, l, depth, N_META)
        kv_prompt = (k_all, v_all)
        y_c, conv_st = _conv(p3, zero_prev, conv_w[l], col_conv, d_conv)
        h_m, c_fin, n_fin, m_fin = _mlstm(p3, g3, b_i[l], b_f[l], zero_c, zero_v, zero_v, col_ml)
        outs["cp"].append(conv_st)
        outs["Cp"].append(c_fin)
        outs["np"].append(n_fin[:, :, 0, :])
        outs["mp"].append(m_fin[:, :, 0, 0])
        xp = mix_and_mlp(xp, h_a.reshape(bp * t_p, d_att), y_c.reshape(bp * t_p, d_conv),
                         h_m.reshape(bp * t_p, d_ml), l)

        p, g = _proj(xs, w_main[l], w_gate[l], tn_proj)
        p3 = p.reshape(bs, n_dec, d_main)
        g3 = g.reshape(bs, n_dec, LANES)
        q_s = p3[:, :, :d_att].reshape(bs, n_dec, h_att, HEAD_DIM)
        k_s = p3[:, :, d_att:2 * d_att].reshape(bs, n_dec, h_att, HEAD_DIM)
        v_s = p3[:, :, 2 * d_att:3 * d_att].reshape(bs, n_dec, h_att, HEAD_DIM)
        h_a = _attn_sample(q_s, k_s, v_s, cache_k, cache_v, page_table, sb_bias[l], u2, l)
        prev8 = jnp.pad(state_conv[l], ((0, 0), (SUBLANES - (CONV_WIDTH - 1), 0), (0, 0)))
        y_c, conv_st = _conv(p3, prev8, conv_w[l], col_conv, d_conv)
        h_m, c_fin, n_fin, m_fin = _mlstm(p3, g3, b_i[l], b_f[l], state_C[l],
                                          state_n[l][:, :, None, :], lane_bcast(state_m[l]), col_ml)
        outs["ks"].append(k_s)
        outs["vs"].append(v_s)
        outs["cs"].append(conv_st)
        outs["Cs"].append(c_fin)
        outs["ns"].append(n_fin[:, :, 0, :])
        outs["ms"].append(m_fin[:, :, 0, 0])
        xs = mix_and_mlp(xs, h_a.reshape(bs * n_dec, d_att), y_c.reshape(bs * n_dec, d_conv),
                         h_m.reshape(bs * n_dec, d_ml), l)

    y_prompt = xp.reshape(bp, t_p, d_model)[:, N_META:]
    y_sample = xs.reshape(bs, n_dec, d_model)
    k_prompt = kv_prompt[0].reshape(depth, bp, t_p, h_att, HEAD_DIM)
    v_prompt = kv_prompt[1].reshape(depth, bp, t_p, h_att, HEAD_DIM)
    st = lambda name: jnp.stack(outs[name])
    return (y_prompt, y_sample, k_prompt, v_prompt, st("cp"), st("Cp"), st("np"), st("mp"),
            st("ks"), st("vs"), st("cs"), st("Cs"), st("ns"), st("ms"))
```

```python
import functools

import jax
import jax.numpy as jnp
from jax import lax
from jax.experimental import pallas as pl
from jax.experimental.pallas import tpu as pltpu

F32 = jnp.float32
BF16 = jnp.bfloat16

HEAD_DIM = 128
N_META = 16
PAGE_SIZE = 128
CONV_WIDTH = 3
LN_EPS = 1e-5
RMS_EPS = 1e-6
SB_SCALE = HEAD_DIM ** -0.5
MLSTM_K_SCALE = HEAD_DIM ** -0.5
NEG_BIG = -1e30

V7X_VMEM_LIMIT_BYTES = 56 * 1024 * 1024
LANES = 128
SUBLANES = 8
BLK = 128


def _cparams(n_axes):
    return pltpu.CompilerParams(dimension_semantics=("arbitrary",) * n_axes,
                                vmem_limit_bytes=V7X_VMEM_LIMIT_BYTES)


def _row_tile(m, cap=704):
    if m <= cap:
        return m
    best = None
    for t in range(16, cap + 1, 16):
        if m % t == 0:
            best = t
    assert best is not None, m
    return best


def _log_sigmoid_pair(z):
    sp = jnp.log(1.0 + jnp.exp(-jnp.abs(z)))
    return jnp.minimum(z, 0.0) - sp, -jnp.maximum(z, 0.0) - sp


def _head_rms(x, g):
    ms = jnp.mean(x * x, axis=-1, keepdims=True)
    return x * lax.rsqrt(ms + RMS_EPS) * g


def _split_bf16(x):
    hi = x.astype(BF16)
    return hi, (x - hi.astype(F32)).astype(BF16)


def _proj_kernel(x_ref, w_ref, wg_ref, p_ref, g_ref, xb_ref):
    @pl.when(pl.program_id(1) == 0)
    def _():
        xb = x_ref[...].astype(BF16)
        xb_ref[...] = xb
        g_ref[...] = jnp.dot(xb, wg_ref[...], preferred_element_type=F32)

    p_ref[...] = jnp.dot(xb_ref[...], w_ref[...], preferred_element_type=F32)


def _proj(x, w, wg, layer, tn):
    m, k = x.shape
    n = w.shape[2]
    tm = _row_tile(m)
    return pl.pallas_call(
        _proj_kernel,
        grid=(m // tm, n // tn),
        in_specs=[pl.BlockSpec((tm, k), lambda i, j: (i, 0)),
                  pl.BlockSpec((None, k, tn), lambda i, j: (layer, 0, j)),
                  pl.BlockSpec((None, k, LANES), lambda i, j: (layer, 0, 0))],
        out_specs=[pl.BlockSpec((tm, tn), lambda i, j: (i, j)),
                   pl.BlockSpec((tm, LANES), lambda i, j: (i, 0))],
        out_shape=[jax.ShapeDtypeStruct((m, n), F32),
                   jax.ShapeDtypeStruct((m, LANES), F32)],
        scratch_shapes=[pltpu.VMEM((tm, k), BF16)],
        compiler_params=_cparams(2),
        name="proj",
    )(x, w, wg)


def _qk(qt, kt):
    return lax.dot_general(qt, kt, (((1,), (1,)), ((), ())), preferred_element_type=F32)


def _sb_scores(s, bias, u2, mask):
    tq, nk = s.shape
    m = nk // BLK
    lsz, lk = _log_sigmoid_pair(s * SB_SCALE + bias)
    if mask is not None:
        lk = jnp.where(mask, lk, 0.0)
    hi, lo = _split_bf16(lk)
    lanes = lambda x, j: x[:, j * BLK:(j + 1) * BLK]
    stacked = jnp.concatenate(
        [jnp.concatenate([lanes(hi, j), lanes(lo, j)], axis=1) for j in range(m)], axis=0)
    cs = jnp.dot(stacked, u2, preferred_element_type=F32)
    pieces, later = [None] * m, None
    for j in range(m - 1, -1, -1):
        sfx, tot = cs[j * tq:(j + 1) * tq, :BLK], cs[j * tq:(j + 1) * tq, BLK:]
        pieces[j] = lanes(lsz, j) + (sfx if later is None else sfx + later)
        later = tot if later is None else later + tot
    return (pieces[0] if m == 1 else jnp.concatenate(pieces, axis=1)), later


def _sb_weighted(logit, carry, vt, mask):
    if carry is not None:
        m = logit.shape[1] // BLK
        logit = logit + (carry if m == 1 else jnp.concatenate([carry] * m, axis=1))
    a = jnp.exp(logit)
    if mask is not None:
        a = jnp.where(mask, a, 0.0)
    return jnp.dot(a.astype(BF16), vt, preferred_element_type=F32)


def _attn_kernel(bias_ref, q_ref, k_ref, v_ref, u2_ref, gmix_ref, o_ref, ko_ref, vo_ref,
                 qb, kb, vb, carry_ref, acc_ref, *s_refs, n_lead, run):
    t = q_ref.shape[0]
    tq = carry_ref.shape[0]
    n_run = tq // run
    n_qblk = (t - n_lead) // tq
    bias = bias_ref[pl.program_id(1)]

    qb[...] = q_ref[...].astype(BF16)
    kb[...] = k_ref[...].astype(BF16)
    vb[...] = v_ref[...].astype(BF16)
    ko_ref[...] = k_ref[...]
    vo_ref[...] = v_ref[...]
    u2 = u2_ref[...]

    row = lax.broadcasted_iota(jnp.int32, (BLK, BLK), 0)
    col = lax.broadcasted_iota(jnp.int32, (BLK, BLK), 1)
    logit, _ = _sb_scores(_qk(qb[0:BLK, :], kb[0:BLK, :]), bias, u2, col < row)
    lead = _sb_weighted(logit, None, vb[0:BLK, :], col < row)[0:n_lead]
    o_ref[0:n_lead, :] = _head_rms(lead, gmix_ref[...]).astype(o_ref.dtype)

    qrow = lax.broadcasted_iota(jnp.int32, (tq, run), 0)
    qcol = lax.broadcasted_iota(jnp.int32, (tq, run), 1)

    def q_block(i, _):
        r0 = pl.multiple_of(n_lead + i * tq, 16)
        qt = qb[pl.ds(r0, tq), :]

        def products(c0):
            return _qk(qt, kb[pl.ds(pl.multiple_of(jnp.maximum(c0, n_lead), 16), run), :])

        def step(s, c0):
            logit, tot = _sb_scores(s, bias, u2, None)
            carry = carry_ref[...]
            acc_ref[...] += _sb_weighted(logit, carry, vb[pl.ds(pl.multiple_of(c0, 16), run), :], None)
            carry_ref[...] = carry + tot

        carry = acc = None
        for d in range(n_run - 1, -1, -1):
            c0 = pl.multiple_of(r0 + d * run, 16)
            mask = qcol + d * run < qrow
            logit, tot = _sb_scores(_qk(qt, kb[pl.ds(c0, run), :]), bias, u2, mask)
            pv = _sb_weighted(logit, carry, vb[pl.ds(c0, run), :], mask)
            acc = pv if acc is None else acc + pv
            carry = tot if carry is None else carry + tot
        carry_ref[...] = carry
        acc_ref[...] = acc

        if n_run % 2 == 0:
            s_even, s_odd = s_refs
            s_even[...] = products(r0 - run)

            def pair(n, _):
                c0 = r0 - (2 * n + 1) * run
                s = s_even[...]
                s_odd[...] = products(c0 - run)
                step(s, c0)
                s = s_odd[...]
                s_even[...] = products(c0 - 2 * run)
                step(s, c0 - run)
                return 0

            lax.fori_loop(0, i * (n_run // 2), pair, 0)
        else:
            def single(n, _):
                c0 = r0 - (n + 1) * run
                step(products(c0), c0)
                return 0

            lax.fori_loop(0, i * n_run, single, 0)

        mask = lax.broadcasted_iota(jnp.int32, (tq, BLK), 1) < n_lead
        logit, _ = _sb_scores(_qk(qt, kb[0:BLK, :]), bias, u2, mask)
        out = acc_ref[...] + _sb_weighted(logit, carry_ref[...], vb[0:BLK, :], mask)
        o_ref[pl.ds(r0, tq), :] = _head_rms(out, gmix_ref[...]).astype(o_ref.dtype)
        return 0

    lax.fori_loop(0, n_qblk, q_block, 0)


def _attn_kernel_aliased(bias_ref, q_ref, k_ref, v_ref, u2_ref, gmix_ref, kprev_ref, vprev_ref,
                         *rest, **kw):
    del kprev_ref, vprev_ref
    _attn_kernel(bias_ref, q_ref, k_ref, v_ref, u2_ref, gmix_ref, *rest, **kw)


def _attn_prompt(p3, sb_bias_l, u2, g_mix_l, kv_prev, layer, depth, n_lead):
    nb, t, _ = p3.shape
    n_heads = sb_bias_l.shape[0]
    assert 0 < n_lead <= BLK <= t and n_lead % 16 == 0
    tq = max(c for c in (512, 256, 128) if (t - n_lead) % c == 0)
    run = min(tq, 256)
    seq = lambda off: pl.BlockSpec((None, t, HEAD_DIM), lambda b, h: (b, 0, off + h))
    kv_spec = pl.BlockSpec((None, None, t, HEAD_DIM), lambda b, h: (layer, b, 0, h))
    kv_shape = jax.ShapeDtypeStruct((depth, nb, t, n_heads * HEAD_DIM), F32)
    in_specs = [pl.BlockSpec(memory_space=pltpu.SMEM), seq(0), seq(n_heads), seq(2 * n_heads),
                pl.BlockSpec((2 * BLK, BLK + LANES), lambda b, h: (0, 0)),
                pl.BlockSpec((1, HEAD_DIM), lambda b, h: (0, h))]
    args = [sb_bias_l, p3, p3, p3, u2, g_mix_l]
    aliases = {}
    body = _attn_kernel
    if kv_prev is not None:
        in_specs += [pl.BlockSpec(memory_space=pl.ANY)] * 2
        args += list(kv_prev)
        aliases = {6: 1, 7: 2}
        body = _attn_kernel_aliased
    return pl.pallas_call(
        functools.partial(body, n_lead=n_lead, run=run),
        grid=(nb, n_heads),
        in_specs=in_specs,
        out_specs=[pl.BlockSpec((None, t, HEAD_DIM), lambda b, h: (b, 0, h)), kv_spec, kv_spec],
        out_shape=[jax.ShapeDtypeStruct((nb, t, n_heads * HEAD_DIM), BF16), kv_shape, kv_shape],
        scratch_shapes=[pltpu.VMEM((t, HEAD_DIM), BF16)] * 3 + [pltpu.VMEM((tq, LANES), F32)] * 2
                       + [pltpu.VMEM((tq, run), F32)] * 2,
        input_output_aliases=aliases,
        compiler_params=_cparams(2),
        name="attn_prompt",
    )(*args)


def _attn_sample_kernel(pt_ref, q_ref, bias_ref, own_ref, new_ref, u2_ref, grow_ref, kn_ref, vn_ref,
                        *refs, n_group):
    del pt_ref
    kc_refs, vc_refs = refs[:n_group], refs[n_group:2 * n_group]
    o_ref, acc_ref, carry_ref = refs[2 * n_group:]
    p = pl.program_id(1)
    n_rows = acc_ref.shape[0]
    n_lanes = own_ref.shape[1]
    n_blk = n_lanes // LANES

    def scores(k_ref, keep):
        k2 = k_ref[...].reshape(n_lanes, HEAD_DIM).astype(BF16)
        s = lax.dot_general(q_ref[...], k2, (((1,), (1,)), ((), ())), preferred_element_type=F32)
        lsz, lk = _log_sigmoid_pair(s * SB_SCALE + bias_ref[...])
        hi, lo = _split_bf16(lk * keep)
        lane_blocks = lambda x: jnp.concatenate(
            [x[:, j * LANES:(j + 1) * LANES] for j in range(n_blk)], axis=0)
        return lsz, jnp.concatenate([lane_blocks(hi), lane_blocks(lo)], axis=1)

    def weighted(lsz, cs, v_ref, keep, acc, carry):
        logits = [None] * n_blk
        for j in range(n_blk - 1, -1, -1):
            rows = slice(j * n_rows, (j + 1) * n_rows)
            logits[j] = cs[rows, :LANES] + carry
            carry = carry + cs[rows, LANES:]
        a = jnp.exp(lsz + jnp.concatenate(logits, axis=1)) * keep
        v2 = v_ref[...].reshape(n_lanes, HEAD_DIM).astype(BF16)
        return acc + jnp.dot(a.astype(BF16), v2, preferred_element_type=F32), carry

    def pages(k_refs, v_refs, keep, acc, carry):
        parts = [scores(k_ref, keep) for k_ref in k_refs]
        cs = jnp.dot(jnp.concatenate([x for _, x in parts], axis=0), u2_ref[...],
                     preferred_element_type=F32)
        per_page = n_blk * n_rows
        for g, v_ref in enumerate(v_refs):
            acc, carry = weighted(parts[g][0], cs[g * per_page:(g + 1) * per_page], v_ref, keep,
                                  acc, carry)
        return acc, carry

    @pl.when(p == 0)
    def _():
        acc, carry = pages([kn_ref], [vn_ref], new_ref[...], jnp.zeros(acc_ref.shape, F32),
                           jnp.zeros(carry_ref.shape, F32))
        acc_ref[...] = acc
        carry_ref[...] = carry

    @pl.when(p > 0)
    def _():
        acc, carry = pages(kc_refs, vc_refs, own_ref[...], acc_ref[...], carry_ref[...])
        acc_ref[...] = acc
        carry_ref[...] = carry

    @pl.when(p == pl.num_programs(1) - 1)
    def _():
        o_ref[...] = _head_rms(acc_ref[...], grow_ref[...])


def _attn_sample(q, k_new, v_new, cache_k, cache_v, page_table, sb_bias_l, u2, g_mix_l, layer):
    nb, n_dec, n_heads, _ = q.shape
    n_pages = page_table.shape[1]
    n_rows = n_heads * n_dec
    n_lanes = PAGE_SIZE * n_heads
    assert n_rows % 16 == 0 and n_heads == SUBLANES
    n_group = max(c for c in (8, 4, 2, 1) if n_pages % c == 0)
    q_rows = jnp.transpose(q, (0, 2, 1, 3)).reshape(nb, n_rows, HEAD_DIM).astype(BF16)
    row_head = jnp.repeat(jnp.arange(n_heads), n_dec)[:, None]
    row_query = jnp.tile(jnp.arange(n_dec), n_heads)[:, None]
    lane_pos = jnp.repeat(jnp.arange(PAGE_SIZE), n_heads)[None, :]
    lane_head = jnp.tile(jnp.arange(n_heads), PAGE_SIZE)[None, :]
    own = (row_head == lane_head).astype(F32)
    own_new = own * (lane_pos < row_query).astype(F32)
    bias_rows = jnp.broadcast_to(jnp.repeat(sb_bias_l, n_dec)[:, None], (n_rows, n_lanes))
    pad_pos = lambda a: jnp.pad(a, ((0, 0), (0, PAGE_SIZE - n_dec), (0, 0), (0, 0)))
    g_rows = jnp.repeat(g_mix_l[0, :n_heads * HEAD_DIM].reshape(n_heads, HEAD_DIM), n_dec, axis=0)

    def cache_spec(g):
        def index_map(b, p, pt):
            logical = n_pages - 1 - (jnp.maximum(p - 1, 0) * n_group + g)
            return (layer, pt[b, logical], 0, 0, 0)
        return pl.BlockSpec((None, None, PAGE_SIZE, n_heads, HEAD_DIM), index_map)

    new_spec = pl.BlockSpec((None, PAGE_SIZE, n_heads, HEAD_DIM), lambda b, p, pt: (b, 0, 0, 0))
    row_const = pl.BlockSpec((n_rows, n_lanes), lambda b, p, pt: (0, 0))
    cache_specs = [cache_spec(g) for g in range(n_group)]
    out = pl.pallas_call(
        functools.partial(_attn_sample_kernel, n_group=n_group),
        grid_spec=pltpu.PrefetchScalarGridSpec(
            num_scalar_prefetch=1,
            grid=(nb, n_pages // n_group + 1),
            in_specs=[pl.BlockSpec((None, n_rows, HEAD_DIM), lambda b, p, pt: (b, 0, 0)),
                      row_const, row_const, row_const,
                      pl.BlockSpec(u2.shape, lambda b, p, pt: (0, 0)),
                      pl.BlockSpec((n_rows, HEAD_DIM), lambda b, p, pt: (0, 0)),
                      new_spec, new_spec] + cache_specs + cache_specs,
            out_specs=pl.BlockSpec((None, n_rows, HEAD_DIM), lambda b, p, pt: (b, 0, 0)),
            scratch_shapes=[pltpu.VMEM((n_rows, HEAD_DIM), F32), pltpu.VMEM((n_rows, LANES), F32)]),
        out_shape=jax.ShapeDtypeStruct((nb, n_rows, HEAD_DIM), F32),
        compiler_params=_cparams(2),
        name="attn_sample",
    )(page_table, q_rows, bias_rows, own, own_new, u2, g_rows, pad_pos(k_new), pad_pos(v_new),
      *([cache_k] * n_group), *([cache_v] * n_group))
    return jnp.transpose(out.reshape(nb, n_heads, n_dec, HEAD_DIM), (0, 2, 1, 3)).reshape(
        nb, n_dec, n_heads * HEAD_DIM)


def _conv_kernel(bc_ref, cc_ref, xc_ref, prev_ref, w_ref, gmix_ref, y_ref, st_ref, ext_ref):
    t = bc_ref.shape[0]
    te = ext_ref.shape[0]
    u = cc_ref[...] * xc_ref[...]
    ext_ref[0:SUBLANES, :] = prev_ref[...]
    ext_ref[te - SUBLANES:te, :] = jnp.zeros((SUBLANES, u.shape[1]), F32)
    ext_ref[SUBLANES:SUBLANES + t, :] = u
    e0 = ext_ref[...]
    e1 = pltpu.roll(e0, 1, 0)
    e2 = pltpu.roll(e0, 2, 0)
    w = w_ref[...]
    lo, hi = SUBLANES, SUBLANES + t
    y = bc_ref[...] * (w[0:1] * e2[lo:hi] + w[1:2] * e1[lo:hi] + w[2:3] * e0[lo:hi])
    for s in range(y.shape[1] // HEAD_DIM):
        sl = slice(s * HEAD_DIM, (s + 1) * HEAD_DIM)
        y_ref[:, sl] = _head_rms(y[:, sl], gmix_ref[:, sl]).astype(y_ref.dtype)
    st_ref[...] = ext_ref[hi - (CONV_WIDTH - 1):hi, :]


def _conv(p3, prev8, conv_w_l, g_mix_l, col0, col_mix, d_conv, out_dtype):
    nb, t, _ = p3.shape
    cb = col0 // d_conv
    te = SUBLANES + pl.cdiv(t, SUBLANES) * SUBLANES
    seq = lambda j: pl.BlockSpec((None, t, d_conv), lambda b: (b, 0, cb + j))
    return pl.pallas_call(
        _conv_kernel,
        grid=(nb,),
        in_specs=[seq(0), seq(1), seq(2),
                  pl.BlockSpec((None, SUBLANES, d_conv), lambda b: (b, 0, 0)),
                  pl.BlockSpec((CONV_WIDTH, d_conv), lambda b: (0, 0)),
                  pl.BlockSpec((1, d_conv), lambda b: (0, col_mix // d_conv))],
        out_specs=[pl.BlockSpec((None, t, d_conv), lambda b: (b, 0, 0)),
                   pl.BlockSpec((None, CONV_WIDTH - 1, d_conv), lambda b: (b, 0, 0))],
        out_shape=[jax.ShapeDtypeStruct((nb, t, d_conv), out_dtype),
                   jax.ShapeDtypeStruct((nb, CONV_WIDTH - 1, d_conv), F32)],
        scratch_shapes=[pltpu.VMEM((te, d_conv), F32)],
        compiler_params=_cparams(1),
        name="conv",
    )(p3, p3, p3, prev8, conv_w_l, g_mix_l)


def _mlstm_kernel(bi_ref, bf_ref, q_ref, k_ref, v_ref, om_ref, g_ref, gmix_ref, c0_ref, n0_ref,
                  m0_ref, h_ref, c_ref, n_ref, m_ref, *pad_refs):
    t = q_ref.shape[0]
    n_heads = c0_ref.shape[0]
    nchunk = pl.cdiv(t, BLK)
    done = (nchunk - 1) * BLK
    srcs = (q_ref, k_ref, v_ref, om_ref, g_ref)
    if t < BLK:
        for src, dst in zip(srcs, pad_refs):
            dst[...] = jnp.zeros(dst.shape, F32)
            dst[0:t, :] = src[...]
        srcs = pad_refs
    qs, ks, vs, oms, gs = srcs
    last_r0 = max(t - BLK, 0)

    c_ref[...] = c0_ref[...]
    n_ref[...] = n0_ref[...]
    m_ref[...] = m0_ref[...]

    row = lax.broadcasted_iota(jnp.int32, (BLK, BLK), 0)
    col = lax.broadcasted_iota(jnp.int32, (BLK, BLK), 1)
    eye = row == col
    causal = col <= row
    tcol = lax.broadcasted_iota(jnp.int32, (BLK, 1), 0)

    def chunk(head, r0, valid, g):
        lanes = slice(head * HEAD_DIM, (head + 1) * HEAD_DIM)
        q = qs[pl.ds(r0, BLK), lanes].astype(BF16)
        k = (ks[pl.ds(r0, BLK), lanes] * MLSTM_K_SCALE).astype(BF16)
        vf = vs[pl.ds(r0, BLK), lanes]
        c0 = c_ref[head]
        n0 = n_ref[head]
        m0 = m_ref[head][:, 0:1]

        im_col = jnp.sum(jnp.where(col == head, g, 0.0), axis=1, keepdims=True)
        fm_col = jnp.sum(jnp.where(col == head + n_heads, g, 0.0), axis=1, keepdims=True)
        ig_col = im_col + bi_ref[head]
        lf_col = _log_sigmoid_pair(fm_col + bf_ref[head])[0]
        if valid is not None:
            ig_col = jnp.where(valid, ig_col, NEG_BIG)
            lf_col = jnp.where(valid, lf_col, 0.0)
        ig_row = jnp.sum(jnp.where(eye, ig_col, 0.0), axis=0, keepdims=True)
        lf_row = jnp.sum(jnp.where(eye, lf_col, 0.0), axis=0, keepdims=True)
        cf_col = jnp.sum(jnp.where(causal, lf_row, 0.0), axis=1, keepdims=True)
        cf_row = jnp.sum(jnp.where(row <= col, lf_col, 0.0), axis=0, keepdims=True)

        log_d = jnp.where(causal, cf_col - cf_row + ig_row, -jnp.inf)
        log_inter = cf_col + m0
        m_t = jnp.maximum(log_inter, jnp.max(log_d, axis=1, keepdims=True))
        w = jnp.exp(log_d - m_t) * _qk(q, k)
        s_inter = jnp.exp(log_inter - m_t)
        num = s_inter * _qk(q, c0.astype(BF16)) + jnp.dot(w.astype(BF16), vf.astype(BF16),
                                                           preferred_element_type=F32)
        qn = jnp.sum(q.astype(F32) * n0, axis=1, keepdims=True)
        den = s_inter * qn + jnp.sum(w, axis=1, keepdims=True)
        h = num / jnp.maximum(jnp.abs(den), jnp.exp(-m_t))
        gated = jax.nn.sigmoid(oms[pl.ds(r0, BLK), lanes]) * h

        cf_last = jnp.sum(lf_row, axis=1, keepdims=True)
        log_w = cf_last - cf_row + ig_row
        m_new = jnp.maximum(cf_last + m0, jnp.max(log_w, axis=1, keepdims=True))
        wk_row = jnp.exp(log_w - m_new)
        decay = jnp.exp(cf_last + m0 - m_new)
        wk_col = jnp.sum(jnp.where(eye, wk_row, 0.0), axis=1, keepdims=True)
        vs_t = (vf * wk_col).T.astype(BF16)
        c_ref[head] = decay * c0 + jnp.dot(vs_t, k, preferred_element_type=F32)
        n_ref[head] = decay * n0 + jnp.sum(k.astype(F32) * wk_col, axis=0, keepdims=True)
        m_ref[head] = jnp.broadcast_to(m_new, (1, LANES))
        return _head_rms(gated, gmix_ref[:, lanes]).astype(h_ref.dtype)

    def body(ci, _):
        r0 = pl.multiple_of(ci * BLK, BLK)
        g = gs[pl.ds(r0, BLK), :]
        for head in range(n_heads):
            h_ref[pl.ds(r0, BLK), head * HEAD_DIM:(head + 1) * HEAD_DIM] = chunk(head, r0, None, g)
        return 0

    lax.fori_loop(0, nchunk - 1, body, 0)
    valid = (tcol + last_r0 >= done) & (tcol + last_r0 < t)
    g = gs[last_r0:last_r0 + BLK, :]
    for head in range(n_heads):
        out = chunk(head, last_r0, valid, g)
        h_ref[done:t, head * HEAD_DIM:(head + 1) * HEAD_DIM] = out[done - last_r0:t - last_r0]


def _mlstm(p3, g3, b_i_l, b_f_l, g_mix_l, c0, n0, m0, col0, col_mix, out_dtype):
    nb, t, _ = p3.shape
    n_heads = c0.shape[1]
    d_ml = n_heads * HEAD_DIM
    cb = col0 // d_ml
    seq = lambda j: pl.BlockSpec((None, t, d_ml), lambda b: (b, 0, cb + j))
    vec_spec = pl.BlockSpec((None, n_heads, 1, HEAD_DIM), lambda b: (b, 0, 0, 0))
    mat_spec = pl.BlockSpec((None, n_heads, HEAD_DIM, HEAD_DIM), lambda b: (b, 0, 0, 0))
    smem = pl.BlockSpec(memory_space=pltpu.SMEM)
    pads = [pltpu.VMEM((BLK, d_ml), F32)] * 4 + [pltpu.VMEM((BLK, LANES), F32)] if t < BLK else []
    return pl.pallas_call(
        _mlstm_kernel,
        grid=(nb,),
        in_specs=[smem, smem, seq(0), seq(1), seq(2), seq(3),
                  pl.BlockSpec((None, t, LANES), lambda b: (b, 0, 0)),
                  pl.BlockSpec((1, d_ml), lambda b: (0, col_mix // d_ml)),
                  mat_spec, vec_spec, vec_spec],
        out_specs=[pl.BlockSpec((None, t, d_ml), lambda b: (b, 0, 0)),
                   mat_spec, vec_spec, vec_spec],
        out_shape=[jax.ShapeDtypeStruct((nb, t, d_ml), out_dtype),
                   jax.ShapeDtypeStruct(c0.shape, F32),
                   jax.ShapeDtypeStruct(n0.shape, F32),
                   jax.ShapeDtypeStruct(m0.shape, F32)],
        scratch_shapes=pads,
        compiler_params=_cparams(1),
        name="mlstm",
    )(b_i_l, b_f_l, p3, p3, p3, p3, g3, g_mix_l, c0, n0, m0)


def _merge_kernel(ha_ref, yc_ref, hm_ref, x_ref, w_ref, g_ref, b_ref, o_ref, r_ref, *, alpha):
    j = pl.program_id(1)
    n_blk, _, tn = r_ref.shape
    r = alpha * x_ref[...]
    k0 = 0
    for src in (ha_ref, yc_ref, hm_ref):
        k1 = k0 + src.shape[1]
        r = r + jnp.dot(src[...].astype(BF16), w_ref[k0:k1, :], preferred_element_type=F32)
        k0 = k1
    r_ref[j] = r

    @pl.when(j == n_blk - 1)
    def _():
        d = n_blk * tn
        tot = sum(jnp.sum(r_ref[s], axis=-1, keepdims=True) for s in range(n_blk))
        mu = tot / d
        sq = sum(jnp.sum((r_ref[s] - mu) ** 2, axis=-1, keepdims=True) for s in range(n_blk))
        inv = lax.rsqrt(sq / d + LN_EPS)
        for s in range(n_blk):
            sl = slice(s * tn, (s + 1) * tn)
            o_ref[:, sl] = (r_ref[s] - mu) * inv * g_ref[:, sl] + b_ref[:, sl]


def _merge(h_att, y_conv, h_ml, x, w_out, layer, ln_g, ln_b, alpha, tn):
    m, d = x.shape
    d_mix = w_out.shape[1]
    tm = _row_tile(m)
    rows = lambda a: pl.BlockSpec((tm, a.shape[1]), lambda i, j: (i, 0))
    vec = lambda a: pl.BlockSpec(a.shape, lambda i, j: (0, 0))
    return pl.pallas_call(
        functools.partial(_merge_kernel, alpha=alpha),
        grid=(m // tm, d // tn),
        in_specs=[rows(h_att), rows(y_conv), rows(h_ml),
                  pl.BlockSpec((tm, tn), lambda i, j: (i, j)),
                  pl.BlockSpec((None, d_mix, tn), lambda i, j: (layer, 0, j)),
                  vec(ln_g), vec(ln_b)],
        out_specs=pl.BlockSpec((tm, d), lambda i, j: (i, 0)),
        out_shape=jax.ShapeDtypeStruct((m, d), F32),
        scratch_shapes=[pltpu.VMEM((d // tn, tm, tn), F32)],
        compiler_params=_cparams(2),
        name="merge",
    )(h_att, y_conv, h_ml, x, w_out, ln_g, ln_b)


def _mlp_kernel(x_ref, wu_ref, wd_ref, g_ref, b_ref, o_ref, xb_ref, acc_ref, *, alpha):
    j = pl.program_id(1)

    @pl.when(j == 0)
    def _():
        x = x_ref[...]
        xb_ref[...] = x.astype(BF16)
        acc_ref[...] = alpha * x

    u = jnp.maximum(jnp.dot(xb_ref[...], wu_ref[...], preferred_element_type=F32), 0.0)
    acc_ref[...] += jnp.dot((u * u).astype(BF16), wd_ref[...], preferred_element_type=F32)

    @pl.when(j == pl.num_programs(1) - 1)
    def _():
        r = acc_ref[...]
        mu = jnp.mean(r, axis=-1, keepdims=True)
        d = r - mu
        var = jnp.mean(d * d, axis=-1, keepdims=True)
        o_ref[...] = d * lax.rsqrt(var + LN_EPS) * g_ref[...] + b_ref[...]


def _mlp(x, w_up, w_down, layer, ln_g, ln_b, alpha, tf):
    m, d = x.shape
    d_ff = w_up.shape[2]
    tm = _row_tile(m)
    vec = pl.BlockSpec((1, d), lambda i, j: (0, 0))
    return pl.pallas_call(
        functools.partial(_mlp_kernel, alpha=alpha),
        grid=(m // tm, d_ff // tf),
        in_specs=[pl.BlockSpec((tm, d), lambda i, j: (i, 0)),
                  pl.BlockSpec((None, d, tf), lambda i, j: (layer, 0, j)),
                  pl.BlockSpec((None, tf, d), lambda i, j: (layer, j, 0)),
                  vec, vec],
        out_specs=pl.BlockSpec((tm, d), lambda i, j: (i, 0)),
        out_shape=jax.ShapeDtypeStruct((m, d), F32),
        scratch_shapes=[pltpu.VMEM((tm, d), BF16), pltpu.VMEM((tm, d), F32)],
        compiler_params=_cparams(2),
        name="mlp",
    )(x, w_up, w_down, ln_g, ln_b)


def kernel(x_prompt, x_sample, cache_k, cache_v, page_table, state_conv, state_C, state_n, state_m,
           meta_tokens, w_in, sb_bias, conv_w, b_i, b_f, g_mix, w_out, ln1_g, ln1_b, w_up, w_down,
           ln2_g, ln2_b):
    depth, d_model, d_in = w_in.shape
    bp, n_seq, _ = x_prompt.shape
    bs, n_dec, _ = x_sample.shape
    n_phys, page, h_att = cache_k.shape[1:4]
    assert page == PAGE_SIZE and n_dec < PAGE_SIZE
    h_ml = b_i.shape[1]
    d_att = h_att * HEAD_DIM
    d_conv = conv_w.shape[2]
    d_ml = h_ml * HEAD_DIM
    d_main = 3 * d_att + 3 * d_conv + 4 * d_ml
    assert d_in == d_main + 2 * h_ml and 2 * h_ml <= LANES
    t_p = N_META + n_seq
    alpha = (2 * depth) ** 0.25
    col_conv = 3 * d_att
    col_ml = col_conv + 3 * d_conv
    tn_proj = d_main // 4
    tn_out = 512
    tf = 512

    w_main = w_in[:, :, :d_main].astype(BF16)
    w_gate = jnp.pad(w_in[:, :, d_main:], ((0, 0), (0, 0), (0, LANES - 2 * h_ml))).astype(BF16)
    w_out_b = w_out.astype(BF16)
    w_up_b = w_up.astype(BF16)
    w_down_b = w_down.astype(BF16)

    pos = jnp.arange(BLK)
    u_strict = (pos[:, None] > pos[None, :]).astype(BF16)
    u_half = jnp.concatenate([u_strict, jnp.ones((BLK, LANES), BF16)], axis=1)
    u2 = jnp.concatenate([u_half, u_half], axis=0)

    row2 = lambda a: a.reshape(1, -1)
    lane_bcast = lambda a: jnp.broadcast_to(a[..., None, None], a.shape + (1, LANES))

    xp = jnp.concatenate([jnp.broadcast_to(meta_tokens[None], (bp, N_META, d_model)), x_prompt],
                         axis=1).reshape(bp * t_p, d_model)
    xs = x_sample.reshape(bs * n_dec, d_model)

    zero_prev = jnp.zeros((bp, SUBLANES, d_conv), F32)
    zero_c = jnp.zeros((bp, h_ml, HEAD_DIM, HEAD_DIM), F32)
    zero_v = jnp.zeros((bp, h_ml, 1, LANES), F32)

    kv_prompt = None
    outs = {name: [] for name in ("cp", "Cp", "np", "mp", "ks", "vs", "cs", "Cs", "ns", "ms")}

    def mix_and_mlp(x, h_a, y_c, h_m, l):
        x1 = _merge(h_a, y_c, h_m, x, w_out_b, l, row2(ln1_g[l]), row2(ln1_b[l]), alpha, tn_out)
        return _mlp(x1, w_up_b, w_down_b, l, row2(ln2_g[l]), row2(ln2_b[l]), alpha, tf)

    for l in range(depth):
        p, g = _proj(xp, w_main, w_gate, l, tn_proj)
        p3 = p.reshape(bp, t_p, d_main)
        g3 = g.reshape(bp, t_p, LANES)
        gm = row2(g_mix[l])
        h_a, k_all, v_all = _attn_prompt(p3, sb_bias[l], u2, gm, kv_prompt, l, depth, N_META)
        kv_prompt = (k_all, v_all)
        y_c, conv_st = _conv(p3, zero_prev, conv_w[l], gm, col_conv, d_att, d_conv, BF16)
        h_m, c_fin, n_fin, m_fin = _mlstm(p3, g3, b_i[l], b_f[l], gm, zero_c, zero_v, zero_v, col_ml,
                                          d_att + d_conv, BF16)
        outs["cp"].append(conv_st)
        outs["Cp"].append(c_fin)
        outs["np"].append(n_fin[:, :, 0, :])
        outs["mp"].append(m_fin[:, :, 0, 0])
        xp = mix_and_mlp(xp, h_a.reshape(bp * t_p, d_att), y_c.reshape(bp * t_p, d_conv),
                         h_m.reshape(bp * t_p, d_ml), l)

        p, g = _proj(xs, w_main, w_gate, l, tn_proj)
        p3 = p.reshape(bs, n_dec, d_main)
        g3 = g.reshape(bs, n_dec, LANES)
        q_s = p3[:, :, :d_att].reshape(bs, n_dec, h_att, HEAD_DIM)
        k_s = p3[:, :, d_att:2 * d_att].reshape(bs, n_dec, h_att, HEAD_DIM)
        v_s = p3[:, :, 2 * d_att:3 * d_att].reshape(bs, n_dec, h_att, HEAD_DIM)
        h_a = _attn_sample(q_s, k_s, v_s, cache_k, cache_v, page_table, sb_bias[l], u2, gm, l)
        prev8 = jnp.pad(state_conv[l], ((0, 0), (SUBLANES - (CONV_WIDTH - 1), 0), (0, 0)))
        y_c, conv_st = _conv(p3, prev8, conv_w[l], gm, col_conv, d_att, d_conv, F32)
        h_m, c_fin, n_fin, m_fin = _mlstm(p3, g3, b_i[l], b_f[l], gm, state_C[l],
                                          state_n[l][:, :, None, :], lane_bcast(state_m[l]), col_ml,
                                          d_att + d_conv, F32)
        outs["ks"].append(k_s)
        outs["vs"].append(v_s)
        outs["cs"].append(conv_st)
        outs["Cs"].append(c_fin)
        outs["ns"].append(n_fin[:, :, 0, :])
        outs["ms"].append(m_fin[:, :, 0, 0])
        xs = mix_and_mlp(xs, h_a.reshape(bs * n_dec, d_att), y_c.reshape(bs * n_dec, d_conv),
                         h_m.reshape(bs * n_dec, d_ml), l)

    y_prompt = xp.reshape(bp, t_p, d_model)[:, N_META:]
    y_sample = xs.reshape(bs, n_dec, d_model)
    k_prompt = kv_prompt[0].reshape(depth, bp, t_p, h_att, HEAD_DIM)
    v_prompt = kv_prompt[1].reshape(depth, bp, t_p, h_att, HEAD_DIM)
    st = lambda name: jnp.stack(outs[name])
    return (y_prompt, y_sample, k_prompt, v_prompt, st("cp"), st("Cp"), st("np"), st("mp"),
            st("ks"), st("vs"), st("cs"), st("Cs"), st("ns"), st("ms"))
```

```python
import functools

import jax
import jax.numpy as jnp
from jax import lax
from jax.experimental import pallas as pl
from jax.experimental.pallas import tpu as pltpu

F32 = jnp.float32
BF16 = jnp.bfloat16

HEAD_DIM = 128
N_META = 16
PAGE_SIZE = 128
CONV_WIDTH = 3
LN_EPS = 1e-5
RMS_EPS = 1e-6
SB_SCALE = HEAD_DIM ** -0.5
MLSTM_K_SCALE = HEAD_DIM ** -0.5
NEG_BIG = -1e30

V7X_VMEM_LIMIT_BYTES = 56 * 1024 * 1024
LANES = 128
SUBLANES = 8
BLK = 128


def _cparams(n_axes):
    return pltpu.CompilerParams(dimension_semantics=("arbitrary",) * n_axes,
                                vmem_limit_bytes=V7X_VMEM_LIMIT_BYTES)


def _row_tile(m, cap=704):
    if m <= cap:
        return m
    best = None
    for t in range(16, cap + 1, 16):
        if m % t == 0:
            best = t
    assert best is not None, m
    return best


def _log_sigmoid_pair(z):
    sp = jnp.log(1.0 + jnp.exp(-jnp.abs(z)))
    return jnp.minimum(z, 0.0) - sp, -jnp.maximum(z, 0.0) - sp


def _head_rms(x, g):
    ms = jnp.mean(x * x, axis=-1, keepdims=True)
    return x * lax.rsqrt(ms + RMS_EPS) * g


def _split_bf16(x):
    hi = x.astype(BF16)
    return hi, (x - hi.astype(F32)).astype(BF16)


def _proj_kernel(x_ref, w_ref, wg_ref, p_ref, g_ref, xb_ref):
    @pl.when(pl.program_id(1) == 0)
    def _():
        xb = x_ref[...].astype(BF16)
        xb_ref[...] = xb
        g_ref[...] = jnp.dot(xb, wg_ref[...], preferred_element_type=F32)

    p_ref[...] = jnp.dot(xb_ref[...], w_ref[...], preferred_element_type=F32)


def _proj(x, w, wg, layer, tn):
    m, k = x.shape
    n = w.shape[2]
    tm = _row_tile(m)
    return pl.pallas_call(
        _proj_kernel,
        grid=(m // tm, n // tn),
        in_specs=[pl.BlockSpec((tm, k), lambda i, j: (i, 0)),
                  pl.BlockSpec((None, k, tn), lambda i, j: (layer, 0, j)),
                  pl.BlockSpec((None, k, LANES), lambda i, j: (layer, 0, 0))],
        out_specs=[pl.BlockSpec((tm, tn), lambda i, j: (i, j)),
                   pl.BlockSpec((tm, LANES), lambda i, j: (i, 0))],
        out_shape=[jax.ShapeDtypeStruct((m, n), F32),
                   jax.ShapeDtypeStruct((m, LANES), F32)],
        scratch_shapes=[pltpu.VMEM((tm, k), BF16)],
        compiler_params=_cparams(2),
        name="proj",
    )(x, w, wg)


def _qk(qt, kt):
    return lax.dot_general(qt, kt, (((1,), (1,)), ((), ())), preferred_element_type=F32)


def _sb_scores(s, bias, u2, mask):
    tq, nk = s.shape
    m = nk // BLK
    lsz, lk = _log_sigmoid_pair(s * SB_SCALE + bias)
    if mask is not None:
        lk = jnp.where(mask, lk, 0.0)
    hi, lo = _split_bf16(lk)
    lanes = lambda x, j: x[:, j * BLK:(j + 1) * BLK]
    stacked = jnp.concatenate(
        [jnp.concatenate([lanes(hi, j), lanes(lo, j)], axis=1) for j in range(m)], axis=0)
    cs = jnp.dot(stacked, u2, preferred_element_type=F32)
    pieces, later = [None] * m, None
    for j in range(m - 1, -1, -1):
        sfx, tot = cs[j * tq:(j + 1) * tq, :BLK], cs[j * tq:(j + 1) * tq, BLK:]
        pieces[j] = lanes(lsz, j) + (sfx if later is None else sfx + later)
        later = tot if later is None else later + tot
    return (pieces[0] if m == 1 else jnp.concatenate(pieces, axis=1)), later


def _sb_weighted(logit, carry, vt, mask):
    if carry is not None:
        m = logit.shape[1] // BLK
        logit = logit + (carry if m == 1 else jnp.concatenate([carry] * m, axis=1))
    a = jnp.exp(logit)
    if mask is not None:
        a = jnp.where(mask, a, 0.0)
    return jnp.dot(a.astype(BF16), vt, preferred_element_type=F32)


def _attn_kernel(bias_ref, q_ref, k_ref, v_ref, u2_ref, gmix_ref, o_ref, ko_ref, vo_ref,
                 qb, kb, vb, carry_ref, acc_ref, *s_refs, n_lead, run):
    t = q_ref.shape[0]
    tq = carry_ref.shape[0]
    n_run = tq // run
    n_qblk = (t - n_lead) // tq
    bias = bias_ref[pl.program_id(1)]

    qb[...] = q_ref[...].astype(BF16)
    kb[...] = k_ref[...].astype(BF16)
    vb[...] = v_ref[...].astype(BF16)
    ko_ref[...] = k_ref[...]
    vo_ref[...] = v_ref[...]
    u2 = u2_ref[...]

    row = lax.broadcasted_iota(jnp.int32, (BLK, BLK), 0)
    col = lax.broadcasted_iota(jnp.int32, (BLK, BLK), 1)
    logit, _ = _sb_scores(_qk(qb[0:BLK, :], kb[0:BLK, :]), bias, u2, col < row)
    lead = _sb_weighted(logit, None, vb[0:BLK, :], col < row)[0:n_lead]
    o_ref[0:n_lead, :] = _head_rms(lead, gmix_ref[...]).astype(o_ref.dtype)

    qrow = lax.broadcasted_iota(jnp.int32, (tq, run), 0)
    qcol = lax.broadcasted_iota(jnp.int32, (tq, run), 1)

    def q_block(i, _):
        r0 = pl.multiple_of(n_lead + i * tq, 16)
        qt = qb[pl.ds(r0, tq), :]

        def products(c0):
            return _qk(qt, kb[pl.ds(pl.multiple_of(jnp.maximum(c0, n_lead), 16), run), :])

        def step(s, c0):
            logit, tot = _sb_scores(s, bias, u2, None)
            carry = carry_ref[...]
            acc_ref[...] += _sb_weighted(logit, carry, vb[pl.ds(pl.multiple_of(c0, 16), run), :], None)
            carry_ref[...] = carry + tot

        carry = acc = None
        for d in range(n_run - 1, -1, -1):
            c0 = pl.multiple_of(r0 + d * run, 16)
            mask = qcol + d * run < qrow
            logit, tot = _sb_scores(_qk(qt, kb[pl.ds(c0, run), :]), bias, u2, mask)
            pv = _sb_weighted(logit, carry, vb[pl.ds(c0, run), :], mask)
            acc = pv if acc is None else acc + pv
            carry = tot if carry is None else carry + tot
        carry_ref[...] = carry
        acc_ref[...] = acc

        if n_run % 2 == 0:
            s_even, s_odd = s_refs
            s_even[...] = products(r0 - run)

            def pair(n, _):
                c0 = r0 - (2 * n + 1) * run
                s = s_even[...]
                s_odd[...] = products(c0 - run)
                step(s, c0)
                s = s_odd[...]
                s_even[...] = products(c0 - 2 * run)
                step(s, c0 - run)
                return 0

            lax.fori_loop(0, i * (n_run // 2), pair, 0)
        else:
            def single(n, _):
                c0 = r0 - (n + 1) * run
                step(products(c0), c0)
                return 0

            lax.fori_loop(0, i * n_run, single, 0)

        mask = lax.broadcasted_iota(jnp.int32, (tq, BLK), 1) < n_lead
        logit, _ = _sb_scores(_qk(qt, kb[0:BLK, :]), bias, u2, mask)
        out = acc_ref[...] + _sb_weighted(logit, carry_ref[...], vb[0:BLK, :], mask)
        o_ref[pl.ds(r0, tq), :] = _head_rms(out, gmix_ref[...]).astype(o_ref.dtype)
        return 0

    lax.fori_loop(0, n_qblk, q_block, 0)


def _attn_kernel_aliased(bias_ref, q_ref, k_ref, v_ref, u2_ref, gmix_ref, kprev_ref, vprev_ref,
                         *rest, **kw):
    del kprev_ref, vprev_ref
    _attn_kernel(bias_ref, q_ref, k_ref, v_ref, u2_ref, gmix_ref, *rest, **kw)


def _attn_prompt(p3, sb_bias_l, u2, g_mix_l, kv_prev, layer, depth, n_lead):
    nb, t, _ = p3.shape
    n_heads = sb_bias_l.shape[0]
    assert 0 < n_lead <= BLK <= t and n_lead % 16 == 0
    tq = max(c for c in (512, 256, 128) if (t - n_lead) % c == 0)
    run = min(tq, 256)
    seq = lambda off: pl.BlockSpec((None, t, HEAD_DIM), lambda b, h: (b, 0, off + h))
    kv_spec = pl.BlockSpec((None, None, t, HEAD_DIM), lambda b, h: (layer, b, 0, h))
    kv_shape = jax.ShapeDtypeStruct((depth, nb, t, n_heads * HEAD_DIM), F32)
    in_specs = [pl.BlockSpec(memory_space=pltpu.SMEM), seq(0), seq(n_heads), seq(2 * n_heads),
                pl.BlockSpec((2 * BLK, BLK + LANES), lambda b, h: (0, 0)),
                pl.BlockSpec((1, HEAD_DIM), lambda b, h: (0, h))]
    args = [sb_bias_l, p3, p3, p3, u2, g_mix_l]
    aliases = {}
    body = _attn_kernel
    if kv_prev is not None:
        in_specs += [pl.BlockSpec(memory_space=pl.ANY)] * 2
        args += list(kv_prev)
        aliases = {6: 1, 7: 2}
        body = _attn_kernel_aliased
    return pl.pallas_call(
        functools.partial(body, n_lead=n_lead, run=run),
        grid=(nb, n_heads),
        in_specs=in_specs,
        out_specs=[pl.BlockSpec((None, t, HEAD_DIM), lambda b, h: (b, 0, h)), kv_spec, kv_spec],
        out_shape=[jax.ShapeDtypeStruct((nb, t, n_heads * HEAD_DIM), BF16), kv_shape, kv_shape],
        scratch_shapes=[pltpu.VMEM((t, HEAD_DIM), BF16)] * 3 + [pltpu.VMEM((tq, LANES), F32)] * 2
                       + [pltpu.VMEM((tq, run), F32)] * 2,
        input_output_aliases=aliases,
        compiler_params=_cparams(2),
        name="attn_prompt",
    )(*args)


def _attn_sample_kernel(pt_ref, q_ref, bias_ref, own_ref, new_ref, u2_ref, grow_ref, kn_ref, vn_ref,
                        *refs, n_group):
    del pt_ref
    kc_refs, vc_refs = refs[:n_group], refs[n_group:2 * n_group]
    o_ref, acc_ref, carry_ref = refs[2 * n_group:]
    p = pl.program_id(1)
    n_rows = acc_ref.shape[0]
    n_lanes = own_ref.shape[1]
    n_blk = n_lanes // LANES

    def scores(k_ref, keep):
        k2 = k_ref[...].reshape(n_lanes, HEAD_DIM).astype(BF16)
        s = lax.dot_general(q_ref[...], k2, (((1,), (1,)), ((), ())), preferred_element_type=F32)
        lsz, lk = _log_sigmoid_pair(s * SB_SCALE + bias_ref[...])
        hi, lo = _split_bf16(lk * keep)
        lane_blocks = lambda x: jnp.concatenate(
            [x[:, j * LANES:(j + 1) * LANES] for j in range(n_blk)], axis=0)
        return lsz, jnp.concatenate([lane_blocks(hi), lane_blocks(lo)], axis=1)

    def weighted(lsz, cs, v_ref, keep, acc, carry):
        logits = [None] * n_blk
        for j in range(n_blk - 1, -1, -1):
            rows = slice(j * n_rows, (j + 1) * n_rows)
            logits[j] = cs[rows, :LANES] + carry
            carry = carry + cs[rows, LANES:]
        a = jnp.exp(lsz + jnp.concatenate(logits, axis=1)) * keep
        v2 = v_ref[...].reshape(n_lanes, HEAD_DIM).astype(BF16)
        return acc + jnp.dot(a.astype(BF16), v2, preferred_element_type=F32), carry

    def pages(k_refs, v_refs, keep, acc, carry):
        parts = [scores(k_ref, keep) for k_ref in k_refs]
        cs = jnp.dot(jnp.concatenate([x for _, x in parts], axis=0), u2_ref[...],
                     preferred_element_type=F32)
        per_page = n_blk * n_rows
        for g, v_ref in enumerate(v_refs):
            acc, carry = weighted(parts[g][0], cs[g * per_page:(g + 1) * per_page], v_ref, keep,
                                  acc, carry)
        return acc, carry

    @pl.when(p == 0)
    def _():
        acc, carry = pages([kn_ref], [vn_ref], new_ref[...], jnp.zeros(acc_ref.shape, F32),
                           jnp.zeros(carry_ref.shape, F32))
        acc_ref[...] = acc
        carry_ref[...] = carry

    @pl.when(p > 0)
    def _():
        acc, carry = pages(kc_refs, vc_refs, own_ref[...], acc_ref[...], carry_ref[...])
        acc_ref[...] = acc
        carry_ref[...] = carry

    @pl.when(p == pl.num_programs(1) - 1)
    def _():
        o_ref[...] = _head_rms(acc_ref[...], grow_ref[...])


def _attn_sample(q, k_new, v_new, cache_k, cache_v, page_table, sb_bias_l, u2, g_mix_l, layer):
    nb, n_dec, n_heads, _ = q.shape
    n_pages = page_table.shape[1]
    n_rows = n_heads * n_dec
    n_lanes = PAGE_SIZE * n_heads
    assert n_rows % 16 == 0 and n_heads == SUBLANES
    n_group = max(c for c in (16, 8, 4, 2, 1) if n_pages % c == 0)
    q_rows = jnp.transpose(q, (0, 2, 1, 3)).reshape(nb, n_rows, HEAD_DIM).astype(BF16)
    row_head = jnp.repeat(jnp.arange(n_heads), n_dec)[:, None]
    row_query = jnp.tile(jnp.arange(n_dec), n_heads)[:, None]
    lane_pos = jnp.repeat(jnp.arange(PAGE_SIZE), n_heads)[None, :]
    lane_head = jnp.tile(jnp.arange(n_heads), PAGE_SIZE)[None, :]
    own = (row_head == lane_head).astype(F32)
    own_new = own * (lane_pos < row_query).astype(F32)
    bias_rows = jnp.broadcast_to(jnp.repeat(sb_bias_l, n_dec)[:, None], (n_rows, n_lanes))
    pad_pos = lambda a: jnp.pad(a, ((0, 0), (0, PAGE_SIZE - n_dec), (0, 0), (0, 0)))
    g_rows = jnp.repeat(g_mix_l[0, :n_heads * HEAD_DIM].reshape(n_heads, HEAD_DIM), n_dec, axis=0)

    def cache_spec(g):
        def index_map(b, p, pt):
            logical = n_pages - 1 - (jnp.maximum(p - 1, 0) * n_group + g)
            return (layer, pt[b, logical], 0, 0, 0)
        return pl.BlockSpec((None, None, PAGE_SIZE, n_heads, HEAD_DIM), index_map)

    new_spec = pl.BlockSpec((None, PAGE_SIZE, n_heads, HEAD_DIM), lambda b, p, pt: (b, 0, 0, 0))
    row_const = pl.BlockSpec((n_rows, n_lanes), lambda b, p, pt: (0, 0))
    cache_specs = [cache_spec(g) for g in range(n_group)]
    out = pl.pallas_call(
        functools.partial(_attn_sample_kernel, n_group=n_group),
        grid_spec=pltpu.PrefetchScalarGridSpec(
            num_scalar_prefetch=1,
            grid=(nb, n_pages // n_group + 1),
            in_specs=[pl.BlockSpec((None, n_rows, HEAD_DIM), lambda b, p, pt: (b, 0, 0)),
                      row_const, row_const, row_const,
                      pl.BlockSpec(u2.shape, lambda b, p, pt: (0, 0)),
                      pl.BlockSpec((n_rows, HEAD_DIM), lambda b, p, pt: (0, 0)),
                      new_spec, new_spec] + cache_specs + cache_specs,
            out_specs=pl.BlockSpec((None, n_rows, HEAD_DIM), lambda b, p, pt: (b, 0, 0)),
            scratch_shapes=[pltpu.VMEM((n_rows, HEAD_DIM), F32), pltpu.VMEM((n_rows, LANES), F32)]),
        out_shape=jax.ShapeDtypeStruct((nb, n_rows, HEAD_DIM), F32),
        compiler_params=_cparams(2),
        name="attn_sample",
    )(page_table, q_rows, bias_rows, own, own_new, u2, g_rows, pad_pos(k_new), pad_pos(v_new),
      *([cache_k] * n_group), *([cache_v] * n_group))
    return jnp.transpose(out.reshape(nb, n_heads, n_dec, HEAD_DIM), (0, 2, 1, 3)).reshape(
        nb, n_dec, n_heads * HEAD_DIM)


def _conv_kernel(bc_ref, cc_ref, xc_ref, prev_ref, w_ref, gmix_ref, y_ref, st_ref, ext_ref):
    t = bc_ref.shape[0]
    te = ext_ref.shape[0]
    u = cc_ref[...] * xc_ref[...]
    ext_ref[0:SUBLANES, :] = prev_ref[...]
    ext_ref[te - SUBLANES:te, :] = jnp.zeros((SUBLANES, u.shape[1]), F32)
    ext_ref[SUBLANES:SUBLANES + t, :] = u
    e0 = ext_ref[...]
    e1 = pltpu.roll(e0, 1, 0)
    e2 = pltpu.roll(e0, 2, 0)
    w = w_ref[...]
    lo, hi = SUBLANES, SUBLANES + t
    y = bc_ref[...] * (w[0:1] * e2[lo:hi] + w[1:2] * e1[lo:hi] + w[2:3] * e0[lo:hi])
    for s in range(y.shape[1] // HEAD_DIM):
        sl = slice(s * HEAD_DIM, (s + 1) * HEAD_DIM)
        y_ref[:, sl] = _head_rms(y[:, sl], gmix_ref[:, sl]).astype(y_ref.dtype)
    st_ref[...] = ext_ref[hi - (CONV_WIDTH - 1):hi, :]


def _conv(p3, prev8, conv_w_l, g_mix_l, col0, col_mix, d_conv, out_dtype):
    nb, t, _ = p3.shape
    cb = col0 // d_conv
    te = SUBLANES + pl.cdiv(t, SUBLANES) * SUBLANES
    seq = lambda j: pl.BlockSpec((None, t, d_conv), lambda b: (b, 0, cb + j))
    return pl.pallas_call(
        _conv_kernel,
        grid=(nb,),
        in_specs=[seq(0), seq(1), seq(2),
                  pl.BlockSpec((None, SUBLANES, d_conv), lambda b: (b, 0, 0)),
                  pl.BlockSpec((CONV_WIDTH, d_conv), lambda b: (0, 0)),
                  pl.BlockSpec((1, d_conv), lambda b: (0, col_mix // d_conv))],
        out_specs=[pl.BlockSpec((None, t, d_conv), lambda b: (b, 0, 0)),
                   pl.BlockSpec((None, CONV_WIDTH - 1, d_conv), lambda b: (b, 0, 0))],
        out_shape=[jax.ShapeDtypeStruct((nb, t, d_conv), out_dtype),
                   jax.ShapeDtypeStruct((nb, CONV_WIDTH - 1, d_conv), F32)],
        scratch_shapes=[pltpu.VMEM((te, d_conv), F32)],
        compiler_params=_cparams(1),
        name="conv",
    )(p3, p3, p3, prev8, conv_w_l, g_mix_l)


def _mlstm_kernel(bi_ref, bf_ref, q_ref, k_ref, v_ref, om_ref, g_ref, gmix_ref, c0_ref, n0_ref,
                  m0_ref, h_ref, c_ref, n_ref, m_ref, *pad_refs, gate_lane):
    t = q_ref.shape[0]
    n_heads = c0_ref.shape[0]
    nchunk = pl.cdiv(t, BLK)
    done = (nchunk - 1) * BLK
    srcs = (q_ref, k_ref, v_ref, om_ref, g_ref)
    if t < BLK:
        for src, dst in zip(srcs, pad_refs):
            dst[...] = jnp.zeros(dst.shape, F32)
            dst[0:t, :] = src[...]
        srcs = pad_refs
    qs, ks, vs, oms, gs = srcs
    last_r0 = max(t - BLK, 0)

    row = lax.broadcasted_iota(jnp.int32, (BLK, BLK), 0)
    col = lax.broadcasted_iota(jnp.int32, (BLK, BLK), 1)
    eye = row == col
    causal = col <= row
    tcol = lax.broadcasted_iota(jnp.int32, (BLK, 1), 0)

    def gates(head, r0, valid, g):
        lanes = slice(head * HEAD_DIM, (head + 1) * HEAD_DIM)
        q = qs[pl.ds(r0, BLK), lanes].astype(BF16)
        k = (ks[pl.ds(r0, BLK), lanes] * MLSTM_K_SCALE).astype(BF16)
        im_col = jnp.sum(jnp.where(col == gate_lane + head, g, 0.0), axis=1, keepdims=True)
        fm_col = jnp.sum(jnp.where(col == gate_lane + n_heads + head, g, 0.0), axis=1, keepdims=True)
        ig_col = im_col + bi_ref[head]
        lf_col = _log_sigmoid_pair(fm_col + bf_ref[head])[0]
        if valid is not None:
            ig_col = jnp.where(valid, ig_col, NEG_BIG)
            lf_col = jnp.where(valid, lf_col, 0.0)
        ig_row = jnp.sum(jnp.where(eye, ig_col, 0.0), axis=0, keepdims=True)
        lf_row = jnp.sum(jnp.where(eye, lf_col, 0.0), axis=0, keepdims=True)
        cf_col = jnp.sum(jnp.where(causal, lf_row, 0.0), axis=1, keepdims=True)
        cf_row = jnp.sum(jnp.where(row <= col, lf_col, 0.0), axis=0, keepdims=True)
        log_d = jnp.where(causal, cf_col - cf_row + ig_row, -jnp.inf)
        cf_last = jnp.sum(lf_row, axis=1, keepdims=True)
        log_w = cf_last - cf_row + ig_row
        return dict(q=q, k=k, vf=vs[pl.ds(r0, BLK), lanes], om=oms[pl.ds(r0, BLK), lanes],
                    log_d=log_d, max_d=jnp.max(log_d, axis=1, keepdims=True), qk=_qk(q, k),
                    cf_col=cf_col, cf_last=cf_last, log_w=log_w,
                    max_w=jnp.max(log_w, axis=1, keepdims=True), lanes=lanes)

    def advance(pre, c0, n0, m0):
        q, k, vf = pre["q"], pre["k"], pre["vf"]
        m0 = m0[:, 0:1]
        log_inter = pre["cf_col"] + m0
        m_t = jnp.maximum(log_inter, pre["max_d"])
        w = jnp.exp(pre["log_d"] - m_t) * pre["qk"]
        s_inter = jnp.exp(log_inter - m_t)
        num = s_inter * _qk(q, c0.astype(BF16)) + jnp.dot(w.astype(BF16), vf.astype(BF16),
                                                           preferred_element_type=F32)
        qn = jnp.sum(q.astype(F32) * n0, axis=1, keepdims=True)
        den = s_inter * qn + jnp.sum(w, axis=1, keepdims=True)
        h = num / jnp.maximum(jnp.abs(den), jnp.exp(-m_t))
        gated = jax.nn.sigmoid(pre["om"]) * h

        m_new = jnp.maximum(pre["cf_last"] + m0, pre["max_w"])
        wk_row = jnp.exp(pre["log_w"] - m_new)
        decay = jnp.exp(pre["cf_last"] + m0 - m_new)
        wk_col = jnp.sum(jnp.where(eye, wk_row, 0.0), axis=1, keepdims=True)
        vs_t = (vf * wk_col).T.astype(BF16)
        c_new = decay * c0 + jnp.dot(vs_t, k, preferred_element_type=F32)
        n_new = decay * n0 + jnp.sum(k.astype(F32) * wk_col, axis=0, keepdims=True)
        out = _head_rms(gated, gmix_ref[:, pre["lanes"]]).astype(h_ref.dtype)
        return out, c_new, n_new, jnp.broadcast_to(m_new, (1, LANES))

    def chunk(r0, valid, g, state):
        pres = [gates(head, r0, valid, g) for head in range(n_heads)]
        results = [advance(pres[head], *state[head]) for head in range(n_heads)]
        return [r[0] for r in results], tuple(r[1:] for r in results)

    def body(ci, state):
        r0 = pl.multiple_of(ci * BLK, BLK)
        outs, state = chunk(r0, None, gs[pl.ds(r0, BLK), :], state)
        for head, out in enumerate(outs):
            h_ref[pl.ds(r0, BLK), head * HEAD_DIM:(head + 1) * HEAD_DIM] = out
        return state

    state = tuple((c0_ref[head], n0_ref[head], m0_ref[head]) for head in range(n_heads))
    state = lax.fori_loop(0, nchunk - 1, body, state)
    valid = (tcol + last_r0 >= done) & (tcol + last_r0 < t)
    outs, state = chunk(last_r0, valid, gs[last_r0:last_r0 + BLK, :], state)
    for head, out in enumerate(outs):
        h_ref[done:t, head * HEAD_DIM:(head + 1) * HEAD_DIM] = out[done - last_r0:t - last_r0]
        c_ref[head], n_ref[head], m_ref[head] = state[head]


def _mlstm(p3, g3, b_i_l, b_f_l, g_mix_l, c0, n0, m0, col0, col_mix, gate_lane, out_dtype):
    nb, t, _ = p3.shape
    n_heads = c0.shape[1]
    d_ml = n_heads * HEAD_DIM
    cb = col0 // d_ml
    seq = lambda j: pl.BlockSpec((None, t, d_ml), lambda b: (b, 0, cb + j))
    vec_spec = pl.BlockSpec((None, n_heads, 1, HEAD_DIM), lambda b: (b, 0, 0, 0))
    mat_spec = pl.BlockSpec((None, n_heads, HEAD_DIM, HEAD_DIM), lambda b: (b, 0, 0, 0))
    smem = pl.BlockSpec(memory_space=pltpu.SMEM)
    pads = [pltpu.VMEM((BLK, d_ml), F32)] * 4 + [pltpu.VMEM((BLK, LANES), F32)] if t < BLK else []
    return pl.pallas_call(
        functools.partial(_mlstm_kernel, gate_lane=gate_lane),
        grid=(nb,),
        in_specs=[smem, smem, seq(0), seq(1), seq(2), seq(3),
                  pl.BlockSpec((None, t, LANES), lambda b: (b, 0, 0)),
                  pl.BlockSpec((1, d_ml), lambda b: (0, col_mix // d_ml)),
                  mat_spec, vec_spec, vec_spec],
        out_specs=[pl.BlockSpec((None, t, d_ml), lambda b: (b, 0, 0)),
                   mat_spec, vec_spec, vec_spec],
        out_shape=[jax.ShapeDtypeStruct((nb, t, d_ml), out_dtype),
                   jax.ShapeDtypeStruct(c0.shape, F32),
                   jax.ShapeDtypeStruct(n0.shape, F32),
                   jax.ShapeDtypeStruct(m0.shape, F32)],
        scratch_shapes=pads,
        compiler_params=_cparams(1),
        name="mlstm",
    )(b_i_l, b_f_l, p3, p3, p3, p3, g3, g_mix_l, c0, n0, m0)


def _merge_kernel(ha_ref, yc_ref, hm_ref, x_ref, w_ref, g_ref, b_ref, o_ref, *, alpha):
    r = alpha * x_ref[...]
    k0 = 0
    for src in (ha_ref, yc_ref, hm_ref):
        k1 = k0 + src.shape[1]
        r = r + jnp.dot(src[...].astype(BF16), w_ref[k0:k1, :], preferred_element_type=F32)
        k0 = k1
    mu = jnp.mean(r, axis=-1, keepdims=True)
    d = r - mu
    var = jnp.mean(d * d, axis=-1, keepdims=True)
    o_ref[...] = d * lax.rsqrt(var + LN_EPS) * g_ref[...] + b_ref[...]


def _merge(h_att, y_conv, h_ml, x, w_out, layer, ln_g, ln_b, alpha):
    m, d = x.shape
    d_mix = w_out.shape[1]
    tm = _row_tile(m)
    rows = lambda a: pl.BlockSpec((tm, a.shape[1]), lambda i: (i, 0))
    vec = lambda a: pl.BlockSpec(a.shape, lambda i: (0, 0))
    return pl.pallas_call(
        functools.partial(_merge_kernel, alpha=alpha),
        grid=(m // tm,),
        in_specs=[rows(h_att), rows(y_conv), rows(h_ml), rows(x),
                  pl.BlockSpec((None, d_mix, d), lambda i: (layer, 0, 0),
                               pipeline_mode=pl.Buffered(1)),
                  vec(ln_g), vec(ln_b)],
        out_specs=pl.BlockSpec((tm, d), lambda i: (i, 0)),
        out_shape=jax.ShapeDtypeStruct((m, d), F32),
        compiler_params=_cparams(1),
        name="merge",
    )(h_att, y_conv, h_ml, x, w_out, ln_g, ln_b)


def _mlp_kernel(x_ref, wu_ref, wd_ref, g_ref, b_ref, o_ref, xb_ref, acc_ref, *, alpha):
    j = pl.program_id(1)

    @pl.when(j == 0)
    def _():
        x = x_ref[...]
        xb_ref[...] = x.astype(BF16)
        acc_ref[...] = alpha * x

    u = jnp.maximum(jnp.dot(xb_ref[...], wu_ref[...], preferred_element_type=F32), 0.0)
    acc_ref[...] += jnp.dot((u * u).astype(BF16), wd_ref[...], preferred_element_type=F32)

    @pl.when(j == pl.num_programs(1) - 1)
    def _():
        r = acc_ref[...]
        mu = jnp.mean(r, axis=-1, keepdims=True)
        d = r - mu
        var = jnp.mean(d * d, axis=-1, keepdims=True)
        o_ref[...] = d * lax.rsqrt(var + LN_EPS) * g_ref[...] + b_ref[...]


def _mlp(x, w_up, w_down, layer, ln_g, ln_b, alpha, tf):
    m, d = x.shape
    d_ff = w_up.shape[2]
    tm = _row_tile(m)
    vec = pl.BlockSpec((1, d), lambda i, j: (0, 0))
    return pl.pallas_call(
        functools.partial(_mlp_kernel, alpha=alpha),
        grid=(m // tm, d_ff // tf),
        in_specs=[pl.BlockSpec((tm, d), lambda i, j: (i, 0)),
                  pl.BlockSpec((None, d, tf), lambda i, j: (layer, 0, j)),
                  pl.BlockSpec((None, tf, d), lambda i, j: (layer, j, 0)),
                  vec, vec],
        out_specs=pl.BlockSpec((tm, d), lambda i, j: (i, 0)),
        out_shape=jax.ShapeDtypeStruct((m, d), F32),
        scratch_shapes=[pltpu.VMEM((tm, d), BF16), pltpu.VMEM((tm, d), F32)],
        compiler_params=_cparams(2),
        name="mlp",
    )(x, w_up, w_down, ln_g, ln_b)


def kernel(x_prompt, x_sample, cache_k, cache_v, page_table, state_conv, state_C, state_n, state_m,
           meta_tokens, w_in, sb_bias, conv_w, b_i, b_f, g_mix, w_out, ln1_g, ln1_b, w_up, w_down,
           ln2_g, ln2_b):
    depth, d_model, d_in = w_in.shape
    bp, n_seq, _ = x_prompt.shape
    bs, n_dec, _ = x_sample.shape
    n_phys, page, h_att = cache_k.shape[1:4]
    assert page == PAGE_SIZE and n_dec < PAGE_SIZE
    h_ml = b_i.shape[1]
    d_att = h_att * HEAD_DIM
    d_conv = conv_w.shape[2]
    d_ml = h_ml * HEAD_DIM
    d_main = 3 * d_att + 3 * d_conv + 4 * d_ml
    assert d_in == d_main + 2 * h_ml and 2 * h_ml <= LANES
    t_p = N_META + n_seq
    alpha = (2 * depth) ** 0.25
    col_conv = 3 * d_att
    col_ml = col_conv + 3 * d_conv
    tn_proj = d_main // 4
    tf = 512

    w_main = w_in[:, :, :d_main].astype(BF16)
    w_gate = w_in[:, :, d_in - LANES:].astype(BF16)
    gate_lane = LANES - 2 * h_ml
    w_out_b = w_out.astype(BF16)
    w_up_b = w_up.astype(BF16)
    w_down_b = w_down.astype(BF16)

    pos = jnp.arange(BLK)
    u_strict = (pos[:, None] > pos[None, :]).astype(BF16)
    u_half = jnp.concatenate([u_strict, jnp.ones((BLK, LANES), BF16)], axis=1)
    u2 = jnp.concatenate([u_half, u_half], axis=0)

    row2 = lambda a: a.reshape(1, -1)
    lane_bcast = lambda a: jnp.broadcast_to(a[..., None, None], a.shape + (1, LANES))

    xp = jnp.concatenate([jnp.broadcast_to(meta_tokens[None], (bp, N_META, d_model)), x_prompt],
                         axis=1).reshape(bp * t_p, d_model)
    xs = x_sample.reshape(bs * n_dec, d_model)

    zero_prev = jnp.zeros((bp, SUBLANES, d_conv), F32)
    zero_c = jnp.zeros((bp, h_ml, HEAD_DIM, HEAD_DIM), F32)
    zero_v = jnp.zeros((bp, h_ml, 1, LANES), F32)

    kv_prompt = None
    outs = {name: [] for name in ("cp", "Cp", "np", "mp", "ks", "vs", "cs", "Cs", "ns", "ms")}

    def mix_and_mlp(x, h_a, y_c, h_m, l):
        x1 = _merge(h_a, y_c, h_m, x, w_out_b, l, row2(ln1_g[l]), row2(ln1_b[l]), alpha)
        return _mlp(x1, w_up_b, w_down_b, l, row2(ln2_g[l]), row2(ln2_b[l]), alpha, tf)

    for l in range(depth):
        p, g = _proj(xp, w_main, w_gate, l, tn_proj)
        p3 = p.reshape(bp, t_p, d_main)
        g3 = g.reshape(bp, t_p, LANES)
        gm = row2(g_mix[l])
        h_a, k_all, v_all = _attn_prompt(p3, sb_bias[l], u2, gm, kv_prompt, l, depth, N_META)
        kv_prompt = (k_all, v_all)
        y_c, conv_st = _conv(p3, zero_prev, conv_w[l], gm, col_conv, d_att, d_conv, BF16)
        h_m, c_fin, n_fin, m_fin = _mlstm(p3, g3, b_i[l], b_f[l], gm, zero_c, zero_v, zero_v, col_ml,
                                          d_att + d_conv, gate_lane, BF16)
        outs["cp"].append(conv_st)
        outs["Cp"].append(c_fin)
        outs["np"].append(n_fin[:, :, 0, :])
        outs["mp"].append(m_fin[:, :, 0, 0])
        xp = mix_and_mlp(xp, h_a.reshape(bp * t_p, d_att), y_c.reshape(bp * t_p, d_conv),
                         h_m.reshape(bp * t_p, d_ml), l)

        p, g = _proj(xs, w_main, w_gate, l, tn_proj)
        p3 = p.reshape(bs, n_dec, d_main)
        g3 = g.reshape(bs, n_dec, LANES)
        q_s = p3[:, :, :d_att].reshape(bs, n_dec, h_att, HEAD_DIM)
        k_s = p3[:, :, d_att:2 * d_att].reshape(bs, n_dec, h_att, HEAD_DIM)
        v_s = p3[:, :, 2 * d_att:3 * d_att].reshape(bs, n_dec, h_att, HEAD_DIM)
        h_a = _attn_sample(q_s, k_s, v_s, cache_k, cache_v, page_table, sb_bias[l], u2, gm, l)
        prev8 = jnp.pad(state_conv[l], ((0, 0), (SUBLANES - (CONV_WIDTH - 1), 0), (0, 0)))
        y_c, conv_st = _conv(p3, prev8, conv_w[l], gm, col_conv, d_att, d_conv, F32)
        h_m, c_fin, n_fin, m_fin = _mlstm(p3, g3, b_i[l], b_f[l], gm, state_C[l],
                                          state_n[l][:, :, None, :], lane_bcast(state_m[l]), col_ml,
                                          d_att + d_conv, gate_lane, F32)
        outs["ks"].append(k_s)
        outs["vs"].append(v_s)
        outs["cs"].append(conv_st)
        outs["Cs"].append(c_fin)
        outs["ns"].append(n_fin[:, :, 0, :])
        outs["ms"].append(m_fin[:, :, 0, 0])
        xs = mix_and_mlp(xs, h_a.reshape(bs * n_dec, d_att), y_c.reshape(bs * n_dec, d_conv),
                         h_m.reshape(bs * n_dec, d_ml), l)

    y_prompt = xp.reshape(bp, t_p, d_model)[:, N_META:]
    y_sample = xs.reshape(bs, n_dec, d_model)
    k_prompt = kv_prompt[0].reshape(depth, bp, t_p, h_att, HEAD_DIM)
    v_prompt = kv_prompt[1].reshape(depth, bp, t_p, h_att, HEAD_DIM)
    st = lambda name: jnp.stack(outs[name])
    return (y_prompt, y_sample, k_prompt, v_prompt, st("cp"), st("Cp"), st("np"), st("mp"),
            st("ks"), st("vs"), st("cs"), st("Cs"), st("ns"), st("ms"))
```

```python
import functools

import jax
import jax.numpy as jnp
from jax import lax
from jax.experimental import pallas as pl
from jax.experimental.pallas import tpu as pltpu

F32 = jnp.float32
BF16 = jnp.bfloat16

HEAD_DIM = 128
N_META = 16
PAGE_SIZE = 128
CONV_WIDTH = 3
LN_EPS = 1e-5
RMS_EPS = 1e-6
SB_SCALE = HEAD_DIM ** -0.5
MLSTM_K_SCALE = HEAD_DIM ** -0.5
NEG_BIG = -1e30

V7X_VMEM_LIMIT_BYTES = 56 * 1024 * 1024
LANES = 128
SUBLANES = 8
BLK = 128


def _cparams(n_axes):
    return pltpu.CompilerParams(dimension_semantics=("arbitrary",) * n_axes,
                                vmem_limit_bytes=V7X_VMEM_LIMIT_BYTES)


def _row_tile(m, cap=704):
    if m <= cap:
        return m
    best = None
    for t in range(16, cap + 1, 16):
        if m % t == 0:
            best = t
    assert best is not None, m
    return best


def _log_sigmoid_pair(z):
    sp = jnp.log(1.0 + jnp.exp(-jnp.abs(z)))
    return jnp.minimum(z, 0.0) - sp, -jnp.maximum(z, 0.0) - sp


def _head_rms(x, g):
    ms = jnp.mean(x * x, axis=-1, keepdims=True)
    return x * lax.rsqrt(ms + RMS_EPS) * g


def _split_bf16(x):
    hi = x.astype(BF16)
    return hi, (x - hi.astype(F32)).astype(BF16)


def _proj_kernel(x_ref, xs_ref, w_ref, wg_ref, p_ref, g_ref, ps_ref, gs_ref, xb_ref):
    i, j = pl.program_id(0), pl.program_id(1)

    @pl.when(j == 0)
    def _():
        xb = x_ref[...].astype(BF16)
        xb_ref[...] = xb
        g_ref[...] = jnp.dot(xb, wg_ref[...], preferred_element_type=F32)

    p_ref[...] = jnp.dot(xb_ref[...], w_ref[...], preferred_element_type=F32)

    @pl.when(i == 0)
    def _():
        xsb = xs_ref[...].astype(BF16)
        ps_ref[...] = jnp.dot(xsb, w_ref[...], preferred_element_type=F32)

        @pl.when(j == 0)
        def _():
            gs_ref[...] = jnp.dot(xsb, wg_ref[...], preferred_element_type=F32)


def _proj(x, xs, w, wg, layer, n, tn):
    m, k = x.shape
    ms = xs.shape[0]
    tm = _row_tile(m)
    n_col = n // tn
    return pl.pallas_call(
        _proj_kernel,
        grid=(m // tm, n_col),
        in_specs=[pl.BlockSpec((tm, k), lambda i, j: (i, 0)),
                  pl.BlockSpec((ms, k), lambda i, j: (0, 0)),
                  pl.BlockSpec((None, k, tn), lambda i, j: (layer, 0, j)),
                  pl.BlockSpec((None, k, LANES), lambda i, j: (layer, 0, 0))],
        out_specs=[pl.BlockSpec((tm, tn), lambda i, j: (i, j)),
                   pl.BlockSpec((tm, LANES), lambda i, j: (i, 0)),
                   pl.BlockSpec((ms, tn), lambda i, j: (0, jnp.where(i == 0, j, n_col - 1))),
                   pl.BlockSpec((ms, LANES), lambda i, j: (0, 0))],
        out_shape=[jax.ShapeDtypeStruct((m, n), F32),
                   jax.ShapeDtypeStruct((m, LANES), F32),
                   jax.ShapeDtypeStruct((ms, n), F32),
                   jax.ShapeDtypeStruct((ms, LANES), F32)],
        scratch_shapes=[pltpu.VMEM((tm, k), BF16)],
        compiler_params=_cparams(2),
        name="proj",
    )(x, xs, w, wg)


def _qk(qt, kt):
    return lax.dot_general(qt, kt, (((1,), (1,)), ((), ())), preferred_element_type=F32)


def _sb_scores(s, bias, u2, mask):
    tq, nk = s.shape
    m = nk // BLK
    lsz, lk = _log_sigmoid_pair(s * SB_SCALE + bias)
    if mask is not None:
        lk = jnp.where(mask, lk, 0.0)
    hi, lo = _split_bf16(lk)
    lanes = lambda x, j: x[:, j * BLK:(j + 1) * BLK]
    stacked = jnp.concatenate(
        [jnp.concatenate([lanes(hi, j), lanes(lo, j)], axis=1) for j in range(m)], axis=0)
    cs = jnp.dot(stacked, u2, preferred_element_type=F32)
    pieces, later = [None] * m, None
    for j in range(m - 1, -1, -1):
        sfx, tot = cs[j * tq:(j + 1) * tq, :BLK], cs[j * tq:(j + 1) * tq, BLK:]
        pieces[j] = lanes(lsz, j) + (sfx if later is None else sfx + later)
        later = tot if later is None else later + tot
    return (pieces[0] if m == 1 else jnp.concatenate(pieces, axis=1)), later


def _sb_weighted(logit, carry, vt, mask):
    if carry is not None:
        m = logit.shape[1] // BLK
        logit = logit + (carry if m == 1 else jnp.concatenate([carry] * m, axis=1))
    a = jnp.exp(logit)
    if mask is not None:
        a = jnp.where(mask, a, 0.0)
    return jnp.dot(a.astype(BF16), vt, preferred_element_type=F32)


def _attn_kernel(bias_ref, q_ref, k_ref, v_ref, u2_ref, gmix_ref, o_ref, ko_ref, vo_ref,
                 qb, kb, vb, carry_ref, acc_ref, *s_refs, n_lead, run):
    t = q_ref.shape[0]
    tq = carry_ref.shape[0]
    n_run = tq // run
    n_qblk = (t - n_lead) // tq
    bias = bias_ref[pl.program_id(1)]

    qb[...] = q_ref[...].astype(BF16)
    kb[...] = k_ref[...].astype(BF16)
    vb[...] = v_ref[...].astype(BF16)
    ko_ref[...] = k_ref[...]
    vo_ref[...] = v_ref[...]
    u2 = u2_ref[...]

    row = lax.broadcasted_iota(jnp.int32, (BLK, BLK), 0)
    col = lax.broadcasted_iota(jnp.int32, (BLK, BLK), 1)
    logit, _ = _sb_scores(_qk(qb[0:BLK, :], kb[0:BLK, :]), bias, u2, col < row)
    lead = _sb_weighted(logit, None, vb[0:BLK, :], col < row)[0:n_lead]
    o_ref[0:n_lead, :] = _head_rms(lead, gmix_ref[...]).astype(o_ref.dtype)

    qrow = lax.broadcasted_iota(jnp.int32, (tq, run), 0)
    qcol = lax.broadcasted_iota(jnp.int32, (tq, run), 1)

    def q_block(i, _):
        r0 = pl.multiple_of(n_lead + i * tq, 16)
        qt = qb[pl.ds(r0, tq), :]

        def products(c0):
            return _qk(qt, kb[pl.ds(pl.multiple_of(jnp.maximum(c0, n_lead), 16), run), :])

        def step(s, c0):
            logit, tot = _sb_scores(s, bias, u2, None)
            carry = carry_ref[...]
            acc_ref[...] += _sb_weighted(logit, carry, vb[pl.ds(pl.multiple_of(c0, 16), run), :], None)
            carry_ref[...] = carry + tot

        carry = acc = None
        for d in range(n_run - 1, -1, -1):
            c0 = pl.multiple_of(r0 + d * run, 16)
            mask = qcol + d * run < qrow
            logit, tot = _sb_scores(_qk(qt, kb[pl.ds(c0, run), :]), bias, u2, mask)
            pv = _sb_weighted(logit, carry, vb[pl.ds(c0, run), :], mask)
            acc = pv if acc is None else acc + pv
            carry = tot if carry is None else carry + tot
        carry_ref[...] = carry
        acc_ref[...] = acc

        if n_run % 2 == 0:
            s_even, s_odd = s_refs
            s_even[...] = products(r0 - run)

            def pair(n, _):
                c0 = r0 - (2 * n + 1) * run
                s = s_even[...]
                s_odd[...] = products(c0 - run)
                step(s, c0)
                s = s_odd[...]
                s_even[...] = products(c0 - 2 * run)
                step(s, c0 - run)
                return 0

            lax.fori_loop(0, i * (n_run // 2), pair, 0)
        else:
            def single(n, _):
                c0 = r0 - (n + 1) * run
                step(products(c0), c0)
                return 0

            lax.fori_loop(0, i * n_run, single, 0)

        mask = lax.broadcasted_iota(jnp.int32, (tq, BLK), 1) < n_lead
        logit, _ = _sb_scores(_qk(qt, kb[0:BLK, :]), bias, u2, mask)
        out = acc_ref[...] + _sb_weighted(logit, carry_ref[...], vb[0:BLK, :], mask)
        o_ref[pl.ds(r0, tq), :] = _head_rms(out, gmix_ref[...]).astype(o_ref.dtype)
        return 0

    lax.fori_loop(0, n_qblk, q_block, 0)


def _attn_kernel_aliased(bias_ref, q_ref, k_ref, v_ref, u2_ref, gmix_ref, kprev_ref, vprev_ref,
                         *rest, **kw):
    del kprev_ref, vprev_ref
    _attn_kernel(bias_ref, q_ref, k_ref, v_ref, u2_ref, gmix_ref, *rest, **kw)


def _attn_prompt(p3, sb_bias_l, u2, g_mix_l, kv_prev, layer, depth, n_lead):
    nb, t, _ = p3.shape
    n_heads = sb_bias_l.shape[0]
    assert 0 < n_lead <= BLK <= t and n_lead % 16 == 0
    tq = max(c for c in (512, 256, 128) if (t - n_lead) % c == 0)
    run = min(tq, 256)
    seq = lambda off: pl.BlockSpec((None, t, HEAD_DIM), lambda b, h: (b, 0, off + h))
    kv_spec = pl.BlockSpec((None, None, t, HEAD_DIM), lambda b, h: (layer, b, 0, h))
    kv_shape = jax.ShapeDtypeStruct((depth, nb, t, n_heads * HEAD_DIM), F32)
    in_specs = [pl.BlockSpec(memory_space=pltpu.SMEM), seq(0), seq(n_heads), seq(2 * n_heads),
                pl.BlockSpec((2 * BLK, BLK + LANES), lambda b, h: (0, 0)),
                pl.BlockSpec((1, HEAD_DIM), lambda b, h: (0, h))]
    args = [sb_bias_l, p3, p3, p3, u2, g_mix_l]
    aliases = {}
    body = _attn_kernel
    if kv_prev is not None:
        in_specs += [pl.BlockSpec(memory_space=pl.ANY)] * 2
        args += list(kv_prev)
        aliases = {6: 1, 7: 2}
        body = _attn_kernel_aliased
    return pl.pallas_call(
        functools.partial(body, n_lead=n_lead, run=run),
        grid=(nb, n_heads),
        in_specs=in_specs,
        out_specs=[pl.BlockSpec((None, t, HEAD_DIM), lambda b, h: (b, 0, h)), kv_spec, kv_spec],
        out_shape=[jax.ShapeDtypeStruct((nb, t, n_heads * HEAD_DIM), BF16), kv_shape, kv_shape],
        scratch_shapes=[pltpu.VMEM((t, HEAD_DIM), BF16)] * 3 + [pltpu.VMEM((tq, LANES), F32)] * 2
                       + [pltpu.VMEM((tq, run), F32)] * 2,
        input_output_aliases=aliases,
        compiler_params=_cparams(2),
        name="attn_prompt",
    )(*args)


def _attn_sample_kernel(pt_ref, q_ref, bias_ref, own_ref, new_ref, u2_ref, grow_ref, kn_ref, vn_ref,
                        *refs, n_group):
    del pt_ref
    kc_refs, vc_refs = refs[:n_group], refs[n_group:2 * n_group]
    o_ref, acc_ref, carry_ref = refs[2 * n_group:]
    p = pl.program_id(1)
    n_rows = acc_ref.shape[0]
    n_lanes = own_ref.shape[1]
    n_blk = n_lanes // LANES

    def scores(k_ref, keep):
        k2 = k_ref[...].reshape(n_lanes, HEAD_DIM).astype(BF16)
        s = lax.dot_general(q_ref[...], k2, (((1,), (1,)), ((), ())), preferred_element_type=F32)
        lsz, lk = _log_sigmoid_pair(s * SB_SCALE + bias_ref[...])
        hi, lo = _split_bf16(lk * keep)
        lane_blocks = lambda x: jnp.concatenate(
            [x[:, j * LANES:(j + 1) * LANES] for j in range(n_blk)], axis=0)
        return lsz, jnp.concatenate([lane_blocks(hi), lane_blocks(lo)], axis=1)

    def weighted(lsz, cs, v_ref, keep, acc, carry):
        logits = [None] * n_blk
        for j in range(n_blk - 1, -1, -1):
            rows = slice(j * n_rows, (j + 1) * n_rows)
            logits[j] = cs[rows, :LANES] + carry
            carry = carry + cs[rows, LANES:]
        a = jnp.exp(lsz + jnp.concatenate(logits, axis=1)) * keep
        v2 = v_ref[...].reshape(n_lanes, HEAD_DIM).astype(BF16)
        return acc + jnp.dot(a.astype(BF16), v2, preferred_element_type=F32), carry

    def pages(k_refs, v_refs, keep, acc, carry):
        parts = [scores(k_ref, keep) for k_ref in k_refs]
        cs = jnp.dot(jnp.concatenate([x for _, x in parts], axis=0), u2_ref[...],
                     preferred_element_type=F32)
        per_page = n_blk * n_rows
        for g, v_ref in enumerate(v_refs):
            acc, carry = weighted(parts[g][0], cs[g * per_page:(g + 1) * per_page], v_ref, keep,
                                  acc, carry)
        return acc, carry

    @pl.when(p == 0)
    def _():
        acc, carry = pages([kn_ref], [vn_ref], new_ref[...], jnp.zeros(acc_ref.shape, F32),
                           jnp.zeros(carry_ref.shape, F32))
        acc_ref[...] = acc
        carry_ref[...] = carry

    @pl.when(p > 0)
    def _():
        acc, carry = pages(kc_refs, vc_refs, own_ref[...], acc_ref[...], carry_ref[...])
        acc_ref[...] = acc
        carry_ref[...] = carry

    @pl.when(p == pl.num_programs(1) - 1)
    def _():
        o_ref[...] = _head_rms(acc_ref[...], grow_ref[...])


def _attn_sample(q, k_new, v_new, cache_k, cache_v, page_table, sb_bias_l, u2, g_mix_l, layer):
    nb, n_dec, n_heads, _ = q.shape
    n_pages = page_table.shape[1]
    n_rows = n_heads * n_dec
    n_lanes = PAGE_SIZE * n_heads
    assert n_rows % 16 == 0 and n_heads == SUBLANES
    n_group = max(c for c in (16, 8, 4, 2, 1) if n_pages % c == 0)
    q_rows = jnp.transpose(q, (0, 2, 1, 3)).reshape(nb, n_rows, HEAD_DIM).astype(BF16)
    row_head = jnp.repeat(jnp.arange(n_heads), n_dec)[:, None]
    row_query = jnp.tile(jnp.arange(n_dec), n_heads)[:, None]
    lane_pos = jnp.repeat(jnp.arange(PAGE_SIZE), n_heads)[None, :]
    lane_head = jnp.tile(jnp.arange(n_heads), PAGE_SIZE)[None, :]
    own = (row_head == lane_head).astype(F32)
    own_new = own * (lane_pos < row_query).astype(F32)
    bias_rows = jnp.broadcast_to(jnp.repeat(sb_bias_l, n_dec)[:, None], (n_rows, n_lanes))
    pad_pos = lambda a: jnp.pad(a, ((0, 0), (0, PAGE_SIZE - n_dec), (0, 0), (0, 0)))
    g_rows = jnp.repeat(g_mix_l[0, :n_heads * HEAD_DIM].reshape(n_heads, HEAD_DIM), n_dec, axis=0)

    def cache_spec(g):
        def index_map(b, p, pt):
            logical = n_pages - 1 - (jnp.maximum(p - 1, 0) * n_group + g)
            return (layer, pt[b, logical], 0, 0, 0)
        return pl.BlockSpec((None, None, PAGE_SIZE, n_heads, HEAD_DIM), index_map)

    new_spec = pl.BlockSpec((None, PAGE_SIZE, n_heads, HEAD_DIM), lambda b, p, pt: (b, 0, 0, 0))
    row_const = pl.BlockSpec((n_rows, n_lanes), lambda b, p, pt: (0, 0))
    cache_specs = [cache_spec(g) for g in range(n_group)]
    out = pl.pallas_call(
        functools.partial(_attn_sample_kernel, n_group=n_group),
        grid_spec=pltpu.PrefetchScalarGridSpec(
            num_scalar_prefetch=1,
            grid=(nb, n_pages // n_group + 1),
            in_specs=[pl.BlockSpec((None, n_rows, HEAD_DIM), lambda b, p, pt: (b, 0, 0)),
                      row_const, row_const, row_const,
                      pl.BlockSpec(u2.shape, lambda b, p, pt: (0, 0)),
                      pl.BlockSpec((n_rows, HEAD_DIM), lambda b, p, pt: (0, 0)),
                      new_spec, new_spec] + cache_specs + cache_specs,
            out_specs=pl.BlockSpec((None, n_rows, HEAD_DIM), lambda b, p, pt: (b, 0, 0)),
            scratch_shapes=[pltpu.VMEM((n_rows, HEAD_DIM), F32), pltpu.VMEM((n_rows, LANES), F32)]),
        out_shape=jax.ShapeDtypeStruct((nb, n_rows, HEAD_DIM), F32),
        compiler_params=_cparams(2),
        name="attn_sample",
    )(page_table, q_rows, bias_rows, own, own_new, u2, g_rows, pad_pos(k_new), pad_pos(v_new),
      *([cache_k] * n_group), *([cache_v] * n_group))
    return jnp.transpose(out.reshape(nb, n_heads, n_dec, HEAD_DIM), (0, 2, 1, 3)).reshape(
        nb, n_dec, n_heads * HEAD_DIM)


def _conv_kernel(bc_ref, cc_ref, xc_ref, prev_ref, w_ref, gmix_ref, y_ref, st_ref, ext_ref):
    t = bc_ref.shape[0]
    te = ext_ref.shape[0]
    u = cc_ref[...] * xc_ref[...]
    ext_ref[0:SUBLANES, :] = prev_ref[...]
    ext_ref[te - SUBLANES:te, :] = jnp.zeros((SUBLANES, u.shape[1]), F32)
    ext_ref[SUBLANES:SUBLANES + t, :] = u
    e0 = ext_ref[...]
    e1 = pltpu.roll(e0, 1, 0)
    e2 = pltpu.roll(e0, 2, 0)
    w = w_ref[...]
    lo, hi = SUBLANES, SUBLANES + t
    y = bc_ref[...] * (w[0:1] * e2[lo:hi] + w[1:2] * e1[lo:hi] + w[2:3] * e0[lo:hi])
    for s in range(y.shape[1] // HEAD_DIM):
        sl = slice(s * HEAD_DIM, (s + 1) * HEAD_DIM)
        y_ref[:, sl] = _head_rms(y[:, sl], gmix_ref[:, sl]).astype(y_ref.dtype)
    st_ref[...] = ext_ref[hi - (CONV_WIDTH - 1):hi, :]


def _conv(p3, prev8, conv_w_l, g_mix_l, col0, col_mix, d_conv, out_dtype):
    nb, t, _ = p3.shape
    cb = col0 // d_conv
    te = SUBLANES + pl.cdiv(t, SUBLANES) * SUBLANES
    seq = lambda j: pl.BlockSpec((None, t, d_conv), lambda b: (b, 0, cb + j))
    return pl.pallas_call(
        _conv_kernel,
        grid=(nb,),
        in_specs=[seq(0), seq(1), seq(2),
                  pl.BlockSpec((None, SUBLANES, d_conv), lambda b: (b, 0, 0)),
                  pl.BlockSpec((CONV_WIDTH, d_conv), lambda b: (0, 0)),
                  pl.BlockSpec((1, d_conv), lambda b: (0, col_mix // d_conv))],
        out_specs=[pl.BlockSpec((None, t, d_conv), lambda b: (b, 0, 0)),
                   pl.BlockSpec((None, CONV_WIDTH - 1, d_conv), lambda b: (b, 0, 0))],
        out_shape=[jax.ShapeDtypeStruct((nb, t, d_conv), out_dtype),
                   jax.ShapeDtypeStruct((nb, CONV_WIDTH - 1, d_conv), F32)],
        scratch_shapes=[pltpu.VMEM((te, d_conv), F32)],
        compiler_params=_cparams(1),
        name="conv",
    )(p3, p3, p3, prev8, conv_w_l, g_mix_l)


def _mlstm_kernel(bi_ref, bf_ref, q_ref, k_ref, v_ref, om_ref, g_ref, gmix_ref, c0_ref, n0_ref,
                  m0_ref, h_ref, c_ref, n_ref, m_ref, *pad_refs, gate_lane):
    t = q_ref.shape[0]
    n_heads = c0_ref.shape[0]
    nchunk = pl.cdiv(t, BLK)
    done = (nchunk - 1) * BLK
    srcs = (q_ref, k_ref, v_ref, om_ref, g_ref)
    if t < BLK:
        for src, dst in zip(srcs, pad_refs):
            dst[...] = jnp.zeros(dst.shape, F32)
            dst[0:t, :] = src[...]
        srcs = pad_refs
    qs, ks, vs, oms, gs = srcs
    last_r0 = max(t - BLK, 0)

    row = lax.broadcasted_iota(jnp.int32, (BLK, BLK), 0)
    col = lax.broadcasted_iota(jnp.int32, (BLK, BLK), 1)
    eye = row == col
    causal = col <= row
    tcol = lax.broadcasted_iota(jnp.int32, (BLK, 1), 0)

    def gates(head, r0, valid, g):
        lanes = slice(head * HEAD_DIM, (head + 1) * HEAD_DIM)
        q = qs[pl.ds(r0, BLK), lanes].astype(BF16)
        k = (ks[pl.ds(r0, BLK), lanes] * MLSTM_K_SCALE).astype(BF16)
        im_col = jnp.sum(jnp.where(col == gate_lane + head, g, 0.0), axis=1, keepdims=True)
        fm_col = jnp.sum(jnp.where(col == gate_lane + n_heads + head, g, 0.0), axis=1, keepdims=True)
        ig_col = im_col + bi_ref[head]
        lf_col = _log_sigmoid_pair(fm_col + bf_ref[head])[0]
        if valid is not None:
            ig_col = jnp.where(valid, ig_col, NEG_BIG)
            lf_col = jnp.where(valid, lf_col, 0.0)
        ig_row = jnp.sum(jnp.where(eye, ig_col, 0.0), axis=0, keepdims=True)
        lf_row = jnp.sum(jnp.where(eye, lf_col, 0.0), axis=0, keepdims=True)
        cf_col = jnp.sum(jnp.where(causal, lf_row, 0.0), axis=1, keepdims=True)
        cf_row = jnp.sum(jnp.where(row <= col, lf_col, 0.0), axis=0, keepdims=True)
        log_d = jnp.where(causal, cf_col - cf_row + ig_row, -jnp.inf)
        cf_last = jnp.sum(lf_row, axis=1, keepdims=True)
        log_w = cf_last - cf_row + ig_row
        return dict(q=q, k=k, vf=vs[pl.ds(r0, BLK), lanes], om=oms[pl.ds(r0, BLK), lanes],
                    log_d=log_d, max_d=jnp.max(log_d, axis=1, keepdims=True), qk=_qk(q, k),
                    cf_col=cf_col, cf_last=cf_last, log_w=log_w,
                    max_w=jnp.max(log_w, axis=1, keepdims=True), lanes=lanes)

    def advance(pre, c0, n0, m0):
        q, k, vf = pre["q"], pre["k"], pre["vf"]
        m0 = m0[:, 0:1]
        log_inter = pre["cf_col"] + m0
        m_t = jnp.maximum(log_inter, pre["max_d"])
        w = jnp.exp(pre["log_d"] - m_t) * pre["qk"]
        s_inter = jnp.exp(log_inter - m_t)
        num = s_inter * _qk(q, c0.astype(BF16)) + jnp.dot(w.astype(BF16), vf.astype(BF16),
                                                           preferred_element_type=F32)
        qn = jnp.sum(q.astype(F32) * n0, axis=1, keepdims=True)
        den = s_inter * qn + jnp.sum(w, axis=1, keepdims=True)
        h = num / jnp.maximum(jnp.abs(den), jnp.exp(-m_t))
        gated = jax.nn.sigmoid(pre["om"]) * h

        m_new = jnp.maximum(pre["cf_last"] + m0, pre["max_w"])
        wk_row = jnp.exp(pre["log_w"] - m_new)
        decay = jnp.exp(pre["cf_last"] + m0 - m_new)
        wk_col = jnp.sum(jnp.where(eye, wk_row, 0.0), axis=1, keepdims=True)
        vs_t = (vf * wk_col).T.astype(BF16)
        c_new = decay * c0 + jnp.dot(vs_t, k, preferred_element_type=F32)
        n_new = decay * n0 + jnp.sum(k.astype(F32) * wk_col, axis=0, keepdims=True)
        out = _head_rms(gated, gmix_ref[:, pre["lanes"]]).astype(h_ref.dtype)
        return out, c_new, n_new, jnp.broadcast_to(m_new, (1, LANES))

    def chunk(r0, valid, g, state):
        pres = [gates(head, r0, valid, g) for head in range(n_heads)]
        results = [advance(pres[head], *state[head]) for head in range(n_heads)]
        return [r[0] for r in results], tuple(r[1:] for r in results)

    def body(ci, state):
        r0 = pl.multiple_of(ci * BLK, BLK)
        outs, state = chunk(r0, None, gs[pl.ds(r0, BLK), :], state)
        for head, out in enumerate(outs):
            h_ref[pl.ds(r0, BLK), head * HEAD_DIM:(head + 1) * HEAD_DIM] = out
        return state

    state = tuple((c0_ref[head], n0_ref[head], m0_ref[head]) for head in range(n_heads))
    state = lax.fori_loop(0, nchunk - 1, body, state)
    valid = (tcol + last_r0 >= done) & (tcol + last_r0 < t)
    outs, state = chunk(last_r0, valid, gs[last_r0:last_r0 + BLK, :], state)
    for head, out in enumerate(outs):
        h_ref[done:t, head * HEAD_DIM:(head + 1) * HEAD_DIM] = out[done - last_r0:t - last_r0]
        c_ref[head], n_ref[head], m_ref[head] = state[head]


def _mlstm(p3, g3, b_i_l, b_f_l, g_mix_l, c0, n0, m0, col0, col_mix, gate_lane, out_dtype):
    nb, t, _ = p3.shape
    n_heads = c0.shape[1]
    d_ml = n_heads * HEAD_DIM
    cb = col0 // d_ml
    seq = lambda j: pl.BlockSpec((None, t, d_ml), lambda b: (b, 0, cb + j))
    vec_spec = pl.BlockSpec((None, n_heads, 1, HEAD_DIM), lambda b: (b, 0, 0, 0))
    mat_spec = pl.BlockSpec((None, n_heads, HEAD_DIM, HEAD_DIM), lambda b: (b, 0, 0, 0))
    smem = pl.BlockSpec(memory_space=pltpu.SMEM)
    pads = [pltpu.VMEM((BLK, d_ml), F32)] * 4 + [pltpu.VMEM((BLK, LANES), F32)] if t < BLK else []
    return pl.pallas_call(
        functools.partial(_mlstm_kernel, gate_lane=gate_lane),
        grid=(nb,),
        in_specs=[smem, smem, seq(0), seq(1), seq(2), seq(3),
                  pl.BlockSpec((None, t, LANES), lambda b: (b, 0, 0)),
                  pl.BlockSpec((1, d_ml), lambda b: (0, col_mix // d_ml)),
                  mat_spec, vec_spec, vec_spec],
        out_specs=[pl.BlockSpec((None, t, d_ml), lambda b: (b, 0, 0)),
                   mat_spec, vec_spec, vec_spec],
        out_shape=[jax.ShapeDtypeStruct((nb, t, d_ml), out_dtype),
                   jax.ShapeDtypeStruct(c0.shape, F32),
                   jax.ShapeDtypeStruct(n0.shape, F32),
                   jax.ShapeDtypeStruct(m0.shape, F32)],
        scratch_shapes=pads,
        compiler_params=_cparams(1),
        name="mlstm",
    )(b_i_l, b_f_l, p3, p3, p3, p3, g3, g_mix_l, c0, n0, m0)


def _merge_rows(ha, yc, hm, x, w_ref, g_ref, b_ref, alpha):
    r = alpha * x
    k0 = 0
    for src in (ha, yc, hm):
        k1 = k0 + src.shape[1]
        r = r + jnp.dot(src.astype(BF16), w_ref[k0:k1, :], preferred_element_type=F32)
        k0 = k1
    mu = jnp.mean(r, axis=-1, keepdims=True)
    d = r - mu
    var = jnp.mean(d * d, axis=-1, keepdims=True)
    return d * lax.rsqrt(var + LN_EPS) * g_ref[...] + b_ref[...]


def _merge_kernel(ha_ref, yc_ref, hm_ref, x_ref, has_ref, ycs_ref, hms_ref, xs_ref, w_ref, g_ref, b_ref,
                  o_ref, os_ref, *, alpha):
    o_ref[...] = _merge_rows(ha_ref[...], yc_ref[...], hm_ref[...], x_ref[...], w_ref, g_ref, b_ref,
                             alpha)

    @pl.when(pl.program_id(0) == 0)
    def _():
        os_ref[...] = _merge_rows(has_ref[...], ycs_ref[...], hms_ref[...], xs_ref[...], w_ref, g_ref,
                                  b_ref, alpha)


def _merge(heads, x, heads_s, xs, w_out, layer, ln_g, ln_b, alpha):
    m, d = x.shape
    ms = xs.shape[0]
    d_mix = w_out.shape[1]
    tm = _row_tile(m)
    rows = lambda a: pl.BlockSpec((tm, a.shape[1]), lambda i: (i, 0))
    whole = lambda a: pl.BlockSpec(a.shape, lambda i: (0, 0))
    return pl.pallas_call(
        functools.partial(_merge_kernel, alpha=alpha),
        grid=(m // tm,),
        in_specs=[rows(a) for a in heads] + [rows(x)] + [whole(a) for a in heads_s] + [whole(xs)]
                 + [pl.BlockSpec((None, d_mix, d), lambda i: (layer, 0, 0), pipeline_mode=pl.Buffered(1)),
                    whole(ln_g), whole(ln_b)],
        out_specs=[pl.BlockSpec((tm, d), lambda i: (i, 0)), pl.BlockSpec((ms, d), lambda i: (0, 0))],
        out_shape=[jax.ShapeDtypeStruct((m, d), F32), jax.ShapeDtypeStruct((ms, d), F32)],
        compiler_params=_cparams(1),
        name="merge",
    )(*heads, x, *heads_s, xs, w_out, ln_g, ln_b)


def _mlp_part(xb, wu_ref, wd_ref):
    u = jnp.maximum(jnp.dot(xb, wu_ref[...], preferred_element_type=F32), 0.0)
    return jnp.dot((u * u).astype(BF16), wd_ref[...], preferred_element_type=F32)


def _ln_rows(r, g_ref, b_ref):
    mu = jnp.mean(r, axis=-1, keepdims=True)
    d = r - mu
    var = jnp.mean(d * d, axis=-1, keepdims=True)
    return d * lax.rsqrt(var + LN_EPS) * g_ref[...] + b_ref[...]


def _mlp_kernel(x_ref, xs_ref, wu_ref, wd_ref, g_ref, b_ref, o_ref, os_ref, xb_ref, acc_ref, accs_ref,
                *, alpha):
    i, j = pl.program_id(0), pl.program_id(1)
    last = pl.num_programs(1) - 1

    @pl.when(j == 0)
    def _():
        x = x_ref[...]
        xb_ref[...] = x.astype(BF16)
        acc_ref[...] = alpha * x

    acc_ref[...] += _mlp_part(xb_ref[...], wu_ref, wd_ref)

    @pl.when(j == last)
    def _():
        o_ref[...] = _ln_rows(acc_ref[...], g_ref, b_ref)

    @pl.when(i == 0)
    def _():
        xs = xs_ref[...]
        part = _mlp_part(xs.astype(BF16), wu_ref, wd_ref)

        @pl.when(j == 0)
        def _():
            accs_ref[...] = alpha * xs + part

        @pl.when(j > 0)
        def _():
            accs_ref[...] += part

        @pl.when(j == last)
        def _():
            os_ref[...] = _ln_rows(accs_ref[...], g_ref, b_ref)


def _mlp(x, xs, w_up, w_down, layer, ln_g, ln_b, alpha, tf):
    m, d = x.shape
    ms = xs.shape[0]
    d_ff = w_up.shape[2]
    tm = _row_tile(m)
    vec = pl.BlockSpec((1, d), lambda i, j: (0, 0))
    return pl.pallas_call(
        functools.partial(_mlp_kernel, alpha=alpha),
        grid=(m // tm, d_ff // tf),
        in_specs=[pl.BlockSpec((tm, d), lambda i, j: (i, 0)),
                  pl.BlockSpec((ms, d), lambda i, j: (0, 0)),
                  pl.BlockSpec((None, d, tf), lambda i, j: (layer, 0, j)),
                  pl.BlockSpec((None, tf, d), lambda i, j: (layer, j, 0)),
                  vec, vec],
        out_specs=[pl.BlockSpec((tm, d), lambda i, j: (i, 0)),
                   pl.BlockSpec((ms, d), lambda i, j: (0, 0))],
        out_shape=[jax.ShapeDtypeStruct((m, d), F32), jax.ShapeDtypeStruct((ms, d), F32)],
        scratch_shapes=[pltpu.VMEM((tm, d), BF16), pltpu.VMEM((tm, d), F32), pltpu.VMEM((ms, d), F32)],
        compiler_params=_cparams(2),
        name="mlp",
    )(x, xs, w_up, w_down, ln_g, ln_b)


def kernel(x_prompt, x_sample, cache_k, cache_v, page_table, state_conv, state_C, state_n, state_m,
           meta_tokens, w_in, sb_bias, conv_w, b_i, b_f, g_mix, w_out, ln1_g, ln1_b, w_up, w_down,
           ln2_g, ln2_b):
    depth, d_model, d_in = w_in.shape
    bp, n_seq, _ = x_prompt.shape
    bs, n_dec, _ = x_sample.shape
    n_phys, page, h_att = cache_k.shape[1:4]
    assert page == PAGE_SIZE and n_dec < PAGE_SIZE
    h_ml = b_i.shape[1]
    d_att = h_att * HEAD_DIM
    d_conv = conv_w.shape[2]
    d_ml = h_ml * HEAD_DIM
    d_main = 3 * d_att + 3 * d_conv + 4 * d_ml
    assert d_in == d_main + 2 * h_ml and 2 * h_ml <= LANES
    t_p = N_META + n_seq
    alpha = (2 * depth) ** 0.25
    col_conv = 3 * d_att
    col_ml = col_conv + 3 * d_conv
    tn_proj = d_main // 4
    tf = 512

    w_in_b = w_in.astype(BF16)
    w_gate = w_in[:, :, d_in - LANES:].astype(BF16)
    gate_lane = LANES - 2 * h_ml
    w_out_b = w_out.astype(BF16)
    w_up_b = w_up.astype(BF16)
    w_down_b = w_down.astype(BF16)

    pos = jnp.arange(BLK)
    u_strict = (pos[:, None] > pos[None, :]).astype(BF16)
    u_half = jnp.concatenate([u_strict, jnp.ones((BLK, LANES), BF16)], axis=1)
    u2 = jnp.concatenate([u_half, u_half], axis=0)

    row2 = lambda a: a.reshape(1, -1)
    lane_bcast = lambda a: jnp.broadcast_to(a[..., None, None], a.shape + (1, LANES))

    xp = jnp.concatenate([jnp.broadcast_to(meta_tokens[None], (bp, N_META, d_model)), x_prompt],
                         axis=1).reshape(bp * t_p, d_model)
    xs = x_sample.reshape(bs * n_dec, d_model)

    zero_prev = jnp.zeros((bp, SUBLANES, d_conv), F32)
    zero_c = jnp.zeros((bp, h_ml, HEAD_DIM, HEAD_DIM), F32)
    zero_v = jnp.zeros((bp, h_ml, 1, LANES), F32)

    kv_prompt = None
    outs = {name: [] for name in ("cp", "Cp", "np", "mp", "ks", "vs", "cs", "Cs", "ns", "ms")}

    for l in range(depth):
        gm = row2(g_mix[l])
        p, g, p_s, g_s = _proj(xp, xs, w_in_b, w_gate, l, d_main, tn_proj)

        p3 = p.reshape(bp, t_p, d_main)
        g3 = g.reshape(bp, t_p, LANES)
        h_a, k_all, v_all = _attn_prompt(p3, sb_bias[l], u2, gm, kv_prompt, l, depth, N_META)
        kv_prompt = (k_all, v_all)
        y_c, conv_st = _conv(p3, zero_prev, conv_w[l], gm, col_conv, d_att, d_conv, BF16)
        h_m, c_fin, n_fin, m_fin = _mlstm(p3, g3, b_i[l], b_f[l], gm, zero_c, zero_v, zero_v, col_ml,
                                          d_att + d_conv, gate_lane, BF16)
        outs["cp"].append(conv_st)
        outs["Cp"].append(c_fin)
        outs["np"].append(n_fin[:, :, 0, :])
        outs["mp"].append(m_fin[:, :, 0, 0])
        heads_p = (h_a.reshape(bp * t_p, d_att), y_c.reshape(bp * t_p, d_conv),
                   h_m.reshape(bp * t_p, d_ml))

        p3 = p_s.reshape(bs, n_dec, d_main)
        g3 = g_s.reshape(bs, n_dec, LANES)
        q_s = p3[:, :, :d_att].reshape(bs, n_dec, h_att, HEAD_DIM)
        k_s = p3[:, :, d_att:2 * d_att].reshape(bs, n_dec, h_att, HEAD_DIM)
        v_s = p3[:, :, 2 * d_att:3 * d_att].reshape(bs, n_dec, h_att, HEAD_DIM)
        h_a = _attn_sample(q_s, k_s, v_s, cache_k, cache_v, page_table, sb_bias[l], u2, gm, l)
        prev8 = jnp.pad(state_conv[l], ((0, 0), (SUBLANES - (CONV_WIDTH - 1), 0), (0, 0)))
        y_c, conv_st = _conv(p3, prev8, conv_w[l], gm, col_conv, d_att, d_conv, F32)
        h_m, c_fin, n_fin, m_fin = _mlstm(p3, g3, b_i[l], b_f[l], gm, state_C[l],
                                          state_n[l][:, :, None, :], lane_bcast(state_m[l]), col_ml,
                                          d_att + d_conv, gate_lane, F32)
        outs["ks"].append(k_s)
        outs["vs"].append(v_s)
        outs["cs"].append(conv_st)
        outs["Cs"].append(c_fin)
        outs["ns"].append(n_fin[:, :, 0, :])
        outs["ms"].append(m_fin[:, :, 0, 0])
        heads_s = (h_a.reshape(bs * n_dec, d_att), y_c.reshape(bs * n_dec, d_conv),
                   h_m.reshape(bs * n_dec, d_ml))

        xp, xs = _merge(heads_p, xp, heads_s, xs, w_out_b, l, row2(ln1_g[l]), row2(ln1_b[l]), alpha)
        xp, xs = _mlp(xp, xs, w_up_b, w_down_b, l, row2(ln2_g[l]), row2(ln2_b[l]), alpha, tf)

    y_prompt = xp.reshape(bp, t_p, d_model)[:, N_META:]
    y_sample = xs.reshape(bs, n_dec, d_model)
    k_prompt = kv_prompt[0].reshape(depth, bp, t_p, h_att, HEAD_DIM)
    v_prompt = kv_prompt[1].reshape(depth, bp, t_p, h_att, HEAD_DIM)
    st = lambda name: jnp.stack(outs[name])
    return (y_prompt, y_sample, k_prompt, v_prompt, st("cp"), st("Cp"), st("np"), st("mp"),
            st("ks"), st("vs"), st("cs"), st("Cs"), st("ns"), st("ms"))
```

```python
import functools

import jax
import jax.numpy as jnp
from jax import lax
from jax.experimental import pallas as pl
from jax.experimental.pallas import tpu as pltpu

F32 = jnp.float32
BF16 = jnp.bfloat16

HEAD_DIM = 128
N_META = 16
PAGE_SIZE = 128
CONV_WIDTH = 3
LN_EPS = 1e-5
RMS_EPS = 1e-6
SB_SCALE = HEAD_DIM ** -0.5
MLSTM_K_SCALE = HEAD_DIM ** -0.5
NEG_BIG = -1e30

V7X_VMEM_LIMIT_BYTES = 56 * 1024 * 1024
LANES = 128
SUBLANES = 8
BLK = 128


def _cparams(n_axes):
    return pltpu.CompilerParams(dimension_semantics=("arbitrary",) * n_axes,
                                vmem_limit_bytes=V7X_VMEM_LIMIT_BYTES)


def _row_tile(m, cap=704):
    if m <= cap:
        return m
    best = None
    for t in range(16, cap + 1, 16):
        if m % t == 0:
            best = t
    assert best is not None, m
    return best


def _log_sigmoid_pair(z):
    sp = jnp.log(1.0 + jnp.exp(-jnp.abs(z)))
    return jnp.minimum(z, 0.0) - sp, -jnp.maximum(z, 0.0) - sp


def _head_rms(x, g):
    ms = jnp.mean(x * x, axis=-1, keepdims=True)
    return x * lax.rsqrt(ms + RMS_EPS) * g


def _qk(qt, kt):
    return lax.dot_general(qt, kt, (((1,), (1,)), ((), ())), preferred_element_type=F32)


def _split_bf16(x):
    hi = x.astype(BF16)
    return hi, (x - hi.astype(F32)).astype(BF16)


def _proj_kernel(x_ref, xs_ref, w_ref, wg_ref, p_ref, g_ref, ps_ref, gs_ref, xb_ref):
    i, j = pl.program_id(0), pl.program_id(1)

    @pl.when(j == 0)
    def _():
        xb = x_ref[...].astype(BF16)
        xb_ref[...] = xb
        g_ref[...] = _qk(xb, wg_ref[...])

    p_ref[...] = _qk(xb_ref[...], w_ref[...])

    @pl.when(i == 0)
    def _():
        xsb = xs_ref[...].astype(BF16)
        ps_ref[...] = _qk(xsb, w_ref[...])

        @pl.when(j == 0)
        def _():
            gs_ref[...] = _qk(xsb, wg_ref[...])


def _proj(x, xs, w, wg, layer, n, tn):
    m, k = x.shape
    ms = xs.shape[0]
    tm = _row_tile(m)
    n_col = n // tn
    return pl.pallas_call(
        _proj_kernel,
        grid=(m // tm, n_col),
        in_specs=[pl.BlockSpec((tm, k), lambda i, j: (i, 0)),
                  pl.BlockSpec((ms, k), lambda i, j: (0, 0)),
                  pl.BlockSpec((None, tn, k), lambda i, j: (layer, j, 0)),
                  pl.BlockSpec((None, LANES, k), lambda i, j: (layer, 0, 0))],
        out_specs=[pl.BlockSpec((tm, tn), lambda i, j: (i, j)),
                   pl.BlockSpec((tm, LANES), lambda i, j: (i, 0)),
                   pl.BlockSpec((ms, tn), lambda i, j: (0, jnp.where(i == 0, j, n_col - 1))),
                   pl.BlockSpec((ms, LANES), lambda i, j: (0, 0))],
        out_shape=[jax.ShapeDtypeStruct((m, n), F32),
                   jax.ShapeDtypeStruct((m, LANES), F32),
                   jax.ShapeDtypeStruct((ms, n), F32),
                   jax.ShapeDtypeStruct((ms, LANES), F32)],
        scratch_shapes=[pltpu.VMEM((tm, k), BF16)],
        compiler_params=_cparams(2),
        name="proj",
    )(x, xs, w, wg)


def _sb_scores(s, bias, u2, mask):
    tq, nk = s.shape
    m = nk // BLK
    lsz, lk = _log_sigmoid_pair(s * SB_SCALE + bias)
    if mask is not None:
        lk = jnp.where(mask, lk, 0.0)
    hi, lo = _split_bf16(lk)
    lanes = lambda x, j: x[:, j * BLK:(j + 1) * BLK]
    stacked = jnp.concatenate(
        [jnp.concatenate([lanes(hi, j), lanes(lo, j)], axis=1) for j in range(m)], axis=0)
    cs = jnp.dot(stacked, u2, preferred_element_type=F32)
    pieces, later = [None] * m, None
    for j in range(m - 1, -1, -1):
        sfx, tot = cs[j * tq:(j + 1) * tq, :BLK], cs[j * tq:(j + 1) * tq, BLK:]
        pieces[j] = lanes(lsz, j) + (sfx if later is None else sfx + later)
        later = tot if later is None else later + tot
    return (pieces[0] if m == 1 else jnp.concatenate(pieces, axis=1)), later


def _sb_weighted(logit, carry, vt, mask):
    if carry is not None:
        m = logit.shape[1] // BLK
        logit = logit + (carry if m == 1 else jnp.concatenate([carry] * m, axis=1))
    a = jnp.exp(logit)
    if mask is not None:
        a = jnp.where(mask, a, 0.0)
    return jnp.dot(a.astype(BF16), vt, preferred_element_type=F32)


def _attn_kernel(bias_ref, q_ref, k_ref, v_ref, u2_ref, gmix_ref, o_ref, ko_ref, vo_ref,
                 qb, kb, vb, carry_ref, acc_ref, *s_refs, n_lead, run):
    t = q_ref.shape[0]
    tq = carry_ref.shape[0]
    n_run = tq // run
    n_qblk = (t - n_lead) // tq
    bias = bias_ref[pl.program_id(1)]

    qb[...] = q_ref[...].astype(BF16)
    kb[...] = k_ref[...].astype(BF16)
    vb[...] = v_ref[...].astype(BF16)
    ko_ref[...] = k_ref[...]
    vo_ref[...] = v_ref[...]
    u2 = u2_ref[...]

    row = lax.broadcasted_iota(jnp.int32, (BLK, BLK), 0)
    col = lax.broadcasted_iota(jnp.int32, (BLK, BLK), 1)
    logit, _ = _sb_scores(_qk(qb[0:BLK, :], kb[0:BLK, :]), bias, u2, col < row)
    lead = _sb_weighted(logit, None, vb[0:BLK, :], col < row)[0:n_lead]
    o_ref[0:n_lead, :] = _head_rms(lead, gmix_ref[...]).astype(o_ref.dtype)

    def q_block(i, _):
        r0 = pl.multiple_of(n_lead + i * tq, 16)
        qt = qb[pl.ds(r0, tq), :]

        def products(c0):
            return _qk(qt, kb[pl.ds(pl.multiple_of(jnp.maximum(c0, n_lead), 16), run), :])

        def step(s, c0):
            logit, tot = _sb_scores(s, bias, u2, None)
            carry = carry_ref[...]
            acc_ref[...] += _sb_weighted(logit, carry, vb[pl.ds(pl.multiple_of(c0, 16), run), :], None)
            carry_ref[...] = carry + tot

        carry = acc = None
        for d in range(n_run - 1, -1, -1):
            c0 = pl.multiple_of(r0 + d * run, 16)
            mask = (lax.broadcasted_iota(jnp.int32, (tq, run), 1) + d * run
                    < lax.broadcasted_iota(jnp.int32, (tq, run), 0))
            logit, tot = _sb_scores(_qk(qt, kb[pl.ds(c0, run), :]), bias, u2, mask)
            pv = _sb_weighted(logit, carry, vb[pl.ds(c0, run), :], mask)
            acc = pv if acc is None else acc + pv
            carry = tot if carry is None else carry + tot
        carry_ref[...] = carry
        acc_ref[...] = acc

        if n_run % 2 == 0:
            s_even, s_odd = s_refs
            s_even[...] = products(r0 - run)

            def pair(n, _):
                c0 = r0 - (2 * n + 1) * run
                s = s_even[...]
                s_odd[...] = products(c0 - run)
                step(s, c0)
                s = s_odd[...]
                s_even[...] = products(c0 - 2 * run)
                step(s, c0 - run)
                return 0

            lax.fori_loop(0, i * (n_run // 2), pair, 0)
        else:
            def single(n, _):
                c0 = r0 - (n + 1) * run
                step(products(c0), c0)
                return 0

            lax.fori_loop(0, i * n_run, single, 0)

        mask = lax.broadcasted_iota(jnp.int32, (tq, BLK), 1) < n_lead
        logit, _ = _sb_scores(_qk(qt, kb[0:BLK, :]), bias, u2, mask)
        out = acc_ref[...] + _sb_weighted(logit, carry_ref[...], vb[0:BLK, :], mask)
        o_ref[pl.ds(r0, tq), :] = _head_rms(out, gmix_ref[...]).astype(o_ref.dtype)
        return 0

    lax.fori_loop(0, n_qblk, q_block, 0)


def _attn_kernel_aliased(bias_ref, q_ref, k_ref, v_ref, u2_ref, gmix_ref, kprev_ref, vprev_ref,
                         *rest, **kw):
    del kprev_ref, vprev_ref
    _attn_kernel(bias_ref, q_ref, k_ref, v_ref, u2_ref, gmix_ref, *rest, **kw)


def _attn_prompt(p3, sb_bias_l, u2, g_mix_l, kv_prev, layer, depth, n_lead):
    nb, t, _ = p3.shape
    n_heads = sb_bias_l.shape[0]
    assert 0 < n_lead <= BLK <= t and n_lead % 16 == 0
    tq = max(c for c in (512, 256, 128) if (t - n_lead) % c == 0)
    run = min(tq, 256)
    seq = lambda off: pl.BlockSpec((None, t, HEAD_DIM), lambda b, h: (b, 0, off + h))
    kv_spec = pl.BlockSpec((None, None, t, HEAD_DIM), lambda b, h: (layer, b, 0, h))
    kv_shape = jax.ShapeDtypeStruct((depth, nb, t, n_heads * HEAD_DIM), F32)
    in_specs = [pl.BlockSpec(memory_space=pltpu.SMEM), seq(0), seq(n_heads), seq(2 * n_heads),
                pl.BlockSpec((2 * BLK, BLK + LANES), lambda b, h: (0, 0)),
                pl.BlockSpec((1, HEAD_DIM), lambda b, h: (0, h))]
    args = [sb_bias_l, p3, p3, p3, u2, g_mix_l]
    aliases = {}
    body = _attn_kernel
    if kv_prev is not None:
        in_specs += [pl.BlockSpec(memory_space=pl.ANY)] * 2
        args += list(kv_prev)
        aliases = {6: 1, 7: 2}
        body = _attn_kernel_aliased
    return pl.pallas_call(
        functools.partial(body, n_lead=n_lead, run=run),
        grid=(nb, n_heads),
        in_specs=in_specs,
        out_specs=[pl.BlockSpec((None, t, HEAD_DIM), lambda b, h: (b, 0, h)), kv_spec, kv_spec],
        out_shape=[jax.ShapeDtypeStruct((nb, t, n_heads * HEAD_DIM), BF16), kv_shape, kv_shape],
        scratch_shapes=[pltpu.VMEM((t, HEAD_DIM), BF16)] * 3 + [pltpu.VMEM((tq, LANES), F32)] * 2
                       + [pltpu.VMEM((tq, run), F32)] * 2,
        input_output_aliases=aliases,
        compiler_params=_cparams(2),
        name="attn_prompt",
    )(*args)


def _attn_sample_kernel(pt_ref, q_ref, bias_ref, own_ref, new_ref, u2_ref, grow_ref, kn_ref, vn_ref,
                        *refs, n_group):
    del pt_ref
    kc_refs, vc_refs = refs[:n_group], refs[n_group:2 * n_group]
    o_ref, acc_ref, carry_ref = refs[2 * n_group:]
    p = pl.program_id(1)
    n_rows = acc_ref.shape[0]
    n_lanes = own_ref.shape[1]
    n_blk = n_lanes // LANES

    def scores(k_ref, keep):
        k2 = k_ref[...].reshape(n_lanes, HEAD_DIM).astype(BF16)
        s = lax.dot_general(q_ref[...], k2, (((1,), (1,)), ((), ())), preferred_element_type=F32)
        lsz, lk = _log_sigmoid_pair(s * SB_SCALE + bias_ref[...])
        hi, lo = _split_bf16(lk * keep)
        lane_blocks = lambda x: jnp.concatenate(
            [x[:, j * LANES:(j + 1) * LANES] for j in range(n_blk)], axis=0)
        return lsz, jnp.concatenate([lane_blocks(hi), lane_blocks(lo)], axis=1)

    def weighted(lsz, cs, v_ref, keep, acc, carry):
        logits = [None] * n_blk
        for j in range(n_blk - 1, -1, -1):
            rows = slice(j * n_rows, (j + 1) * n_rows)
            logits[j] = cs[rows, :LANES] + carry
            carry = carry + cs[rows, LANES:]
        a = jnp.exp(lsz + jnp.concatenate(logits, axis=1)) * keep
        v2 = v_ref[...].reshape(n_lanes, HEAD_DIM).astype(BF16)
        return acc + jnp.dot(a.astype(BF16), v2, preferred_element_type=F32), carry

    def pages(k_refs, v_refs, keep, acc, carry):
        parts = [scores(k_ref, keep) for k_ref in k_refs]
        cs = jnp.dot(jnp.concatenate([x for _, x in parts], axis=0), u2_ref[...],
                     preferred_element_type=F32)
        per_page = n_blk * n_rows
        for g, v_ref in enumerate(v_refs):
            acc, carry = weighted(parts[g][0], cs[g * per_page:(g + 1) * per_page], v_ref, keep,
                                  acc, carry)
        return acc, carry

    @pl.when(p == 0)
    def _():
        acc, carry = pages([kn_ref], [vn_ref], new_ref[...], jnp.zeros(acc_ref.shape, F32),
                           jnp.zeros(carry_ref.shape, F32))
        acc_ref[...] = acc
        carry_ref[...] = carry

    @pl.when(p > 0)
    def _():
        acc, carry = pages(kc_refs, vc_refs, own_ref[...], acc_ref[...], carry_ref[...])
        acc_ref[...] = acc
        carry_ref[...] = carry

    @pl.when(p == pl.num_programs(1) - 1)
    def _():
        o_ref[...] = _head_rms(acc_ref[...], grow_ref[...])


def _attn_sample(q, k_new, v_new, cache_k, cache_v, page_table, sb_bias_l, u2, g_mix_l, layer):
    nb, n_dec, n_heads, _ = q.shape
    n_pages = page_table.shape[1]
    n_rows = n_heads * n_dec
    n_lanes = PAGE_SIZE * n_heads
    assert n_rows % 16 == 0 and n_heads == SUBLANES
    n_group = max(c for c in (16, 8, 4, 2, 1) if n_pages % c == 0)
    q_rows = jnp.transpose(q, (0, 2, 1, 3)).reshape(nb, n_rows, HEAD_DIM).astype(BF16)
    row_head = jnp.repeat(jnp.arange(n_heads), n_dec)[:, None]
    row_query = jnp.tile(jnp.arange(n_dec), n_heads)[:, None]
    lane_pos = jnp.repeat(jnp.arange(PAGE_SIZE), n_heads)[None, :]
    lane_head = jnp.tile(jnp.arange(n_heads), PAGE_SIZE)[None, :]
    own = (row_head == lane_head).astype(F32)
    own_new = own * (lane_pos < row_query).astype(F32)
    bias_rows = jnp.broadcast_to(jnp.repeat(sb_bias_l, n_dec)[:, None], (n_rows, n_lanes))
    pad_pos = lambda a: jnp.pad(a, ((0, 0), (0, PAGE_SIZE - n_dec), (0, 0), (0, 0)))
    g_rows = jnp.repeat(g_mix_l[0, :n_heads * HEAD_DIM].reshape(n_heads, HEAD_DIM), n_dec, axis=0)

    def cache_spec(g):
        def index_map(b, p, pt):
            logical = n_pages - 1 - (jnp.maximum(p - 1, 0) * n_group + g)
            return (layer, pt[b, logical], 0, 0, 0)
        return pl.BlockSpec((None, None, PAGE_SIZE, n_heads, HEAD_DIM), index_map)

    new_spec = pl.BlockSpec((None, PAGE_SIZE, n_heads, HEAD_DIM), lambda b, p, pt: (b, 0, 0, 0))
    row_const = pl.BlockSpec((n_rows, n_lanes), lambda b, p, pt: (0, 0))
    cache_specs = [cache_spec(g) for g in range(n_group)]
    out = pl.pallas_call(
        functools.partial(_attn_sample_kernel, n_group=n_group),
        grid_spec=pltpu.PrefetchScalarGridSpec(
            num_scalar_prefetch=1,
            grid=(nb, n_pages // n_group + 1),
            in_specs=[pl.BlockSpec((None, n_rows, HEAD_DIM), lambda b, p, pt: (b, 0, 0)),
                      row_const, row_const, row_const,
                      pl.BlockSpec(u2.shape, lambda b, p, pt: (0, 0)),
                      pl.BlockSpec((n_rows, HEAD_DIM), lambda b, p, pt: (0, 0)),
                      new_spec, new_spec] + cache_specs + cache_specs,
            out_specs=pl.BlockSpec((None, n_rows, HEAD_DIM), lambda b, p, pt: (b, 0, 0)),
            scratch_shapes=[pltpu.VMEM((n_rows, HEAD_DIM), F32), pltpu.VMEM((n_rows, LANES), F32)]),
        out_shape=jax.ShapeDtypeStruct((nb, n_rows, HEAD_DIM), F32),
        compiler_params=_cparams(2),
        name="attn_sample",
    )(page_table, q_rows, bias_rows, own, own_new, u2, g_rows, pad_pos(k_new), pad_pos(v_new),
      *([cache_k] * n_group), *([cache_v] * n_group))
    return jnp.transpose(out.reshape(nb, n_heads, n_dec, HEAD_DIM), (0, 2, 1, 3)).reshape(
        nb, n_dec, n_heads * HEAD_DIM)


def _conv_kernel(bc_ref, cc_ref, xc_ref, prev_ref, w_ref, gmix_ref, y_ref, st_ref, ext_ref):
    t = bc_ref.shape[0]
    te = ext_ref.shape[0]
    u = cc_ref[...] * xc_ref[...]
    ext_ref[0:SUBLANES, :] = prev_ref[...]
    ext_ref[te - SUBLANES:te, :] = jnp.zeros((SUBLANES, u.shape[1]), F32)
    ext_ref[SUBLANES:SUBLANES + t, :] = u
    e0 = ext_ref[...]
    e1 = pltpu.roll(e0, 1, 0)
    e2 = pltpu.roll(e0, 2, 0)
    w = w_ref[...]
    lo, hi = SUBLANES, SUBLANES + t
    y = bc_ref[...] * (w[0:1] * e2[lo:hi] + w[1:2] * e1[lo:hi] + w[2:3] * e0[lo:hi])
    for s in range(y.shape[1] // HEAD_DIM):
        sl = slice(s * HEAD_DIM, (s + 1) * HEAD_DIM)
        y_ref[:, sl] = _head_rms(y[:, sl], gmix_ref[:, sl]).astype(y_ref.dtype)
    st_ref[...] = ext_ref[hi - (CONV_WIDTH - 1):hi, :]


def _conv(p3, prev8, conv_w_l, g_mix_l, col0, col_mix, d_conv, out_dtype):
    nb, t, _ = p3.shape
    cb = col0 // d_conv
    te = SUBLANES + pl.cdiv(t, SUBLANES) * SUBLANES
    seq = lambda j: pl.BlockSpec((None, t, d_conv), lambda b: (b, 0, cb + j))
    return pl.pallas_call(
        _conv_kernel,
        grid=(nb,),
        in_specs=[seq(0), seq(1), seq(2),
                  pl.BlockSpec((None, SUBLANES, d_conv), lambda b: (b, 0, 0)),
                  pl.BlockSpec((CONV_WIDTH, d_conv), lambda b: (0, 0)),
                  pl.BlockSpec((1, d_conv), lambda b: (0, col_mix // d_conv))],
        out_specs=[pl.BlockSpec((None, t, d_conv), lambda b: (b, 0, 0)),
                   pl.BlockSpec((None, CONV_WIDTH - 1, d_conv), lambda b: (b, 0, 0))],
        out_shape=[jax.ShapeDtypeStruct((nb, t, d_conv), out_dtype),
                   jax.ShapeDtypeStruct((nb, CONV_WIDTH - 1, d_conv), F32)],
        scratch_shapes=[pltpu.VMEM((te, d_conv), F32)],
        compiler_params=_cparams(1),
        name="conv",
    )(p3, p3, p3, prev8, conv_w_l, g_mix_l)


def _mlstm_kernel(bi_ref, bf_ref, q_ref, k_ref, v_ref, om_ref, g_ref, gmix_ref, c0_ref, n0_ref,
                  m0_ref, h_ref, c_ref, n_ref, m_ref, *pad_refs, gate_lane):
    t = q_ref.shape[0]
    n_heads = c0_ref.shape[0]
    nchunk = pl.cdiv(t, BLK)
    done = (nchunk - 1) * BLK
    srcs = (q_ref, k_ref, v_ref, om_ref, g_ref)
    if t < BLK:
        for src, dst in zip(srcs, pad_refs):
            dst[...] = jnp.zeros(dst.shape, F32)
            dst[0:t, :] = src[...]
        srcs = pad_refs
    qs, ks, vs, oms, gs = srcs
    last_r0 = max(t - BLK, 0)

    row = lax.broadcasted_iota(jnp.int32, (BLK, BLK), 0)
    col = lax.broadcasted_iota(jnp.int32, (BLK, BLK), 1)
    eye = row == col
    causal = col <= row
    tcol = lax.broadcasted_iota(jnp.int32, (BLK, 1), 0)

    def gates(head, r0, valid, g):
        lanes = slice(head * HEAD_DIM, (head + 1) * HEAD_DIM)
        q = qs[pl.ds(r0, BLK), lanes].astype(BF16)
        k = (ks[pl.ds(r0, BLK), lanes] * MLSTM_K_SCALE).astype(BF16)
        im_col = jnp.sum(jnp.where(col == gate_lane + head, g, 0.0), axis=1, keepdims=True)
        fm_col = jnp.sum(jnp.where(col == gate_lane + n_heads + head, g, 0.0), axis=1, keepdims=True)
        ig_col = im_col + bi_ref[head]
        lf_col = _log_sigmoid_pair(fm_col + bf_ref[head])[0]
        if valid is not None:
            ig_col = jnp.where(valid, ig_col, NEG_BIG)
            lf_col = jnp.where(valid, lf_col, 0.0)
        ig_row = jnp.sum(jnp.where(eye, ig_col, 0.0), axis=0, keepdims=True)
        lf_row = jnp.sum(jnp.where(eye, lf_col, 0.0), axis=0, keepdims=True)
        cf_col = jnp.sum(jnp.where(causal, lf_row, 0.0), axis=1, keepdims=True)
        cf_row = jnp.sum(jnp.where(row <= col, lf_col, 0.0), axis=0, keepdims=True)
        log_d = jnp.where(causal, cf_col - cf_row + ig_row, -jnp.inf)
        cf_last = jnp.sum(lf_row, axis=1, keepdims=True)
        log_w = cf_last - cf_row + ig_row
        return dict(q=q, k=k, vf=vs[pl.ds(r0, BLK), lanes], om=oms[pl.ds(r0, BLK), lanes],
                    log_d=log_d, max_d=jnp.max(log_d, axis=1, keepdims=True), qk=_qk(q, k),
                    cf_col=cf_col, cf_last=cf_last, log_w=log_w,
                    max_w=jnp.max(log_w, axis=1, keepdims=True), lanes=lanes)

    def advance(pre, c0, n0, m0):
        q, k, vf = pre["q"], pre["k"], pre["vf"]
        m0 = m0[:, 0:1]
        log_inter = pre["cf_col"] + m0
        m_t = jnp.maximum(log_inter, pre["max_d"])
        w = jnp.exp(pre["log_d"] - m_t) * pre["qk"]
        s_inter = jnp.exp(log_inter - m_t)
        num = s_inter * _qk(q, c0.astype(BF16)) + jnp.dot(w.astype(BF16), vf.astype(BF16),
                                                           preferred_element_type=F32)
        qn = jnp.sum(q.astype(F32) * n0, axis=1, keepdims=True)
        den = s_inter * qn + jnp.sum(w, axis=1, keepdims=True)
        h = num / jnp.maximum(jnp.abs(den), jnp.exp(-m_t))
        gated = jax.nn.sigmoid(pre["om"]) * h

        m_new = jnp.maximum(pre["cf_last"] + m0, pre["max_w"])
        wk_row = jnp.exp(pre["log_w"] - m_new)
        decay = jnp.exp(pre["cf_last"] + m0 - m_new)
        wk_col = jnp.sum(jnp.where(eye, wk_row, 0.0), axis=1, keepdims=True)
        vs_t = (vf * wk_col).T.astype(BF16)
        c_new = decay * c0 + jnp.dot(vs_t, k, preferred_element_type=F32)
        n_new = decay * n0 + jnp.sum(k.astype(F32) * wk_col, axis=0, keepdims=True)
        out = _head_rms(gated, gmix_ref[:, pre["lanes"]]).astype(h_ref.dtype)
        return out, c_new, n_new, jnp.broadcast_to(m_new, (1, LANES))

    def chunk(r0, valid, g, state):
        pres = [gates(head, r0, valid, g) for head in range(n_heads)]
        results = [advance(pres[head], *state[head]) for head in range(n_heads)]
        return [r[0] for r in results], tuple(r[1:] for r in results)

    def body(ci, state):
        r0 = pl.multiple_of(ci * BLK, BLK)
        outs, state = chunk(r0, None, gs[pl.ds(r0, BLK), :], state)
        for head, out in enumerate(outs):
            h_ref[pl.ds(r0, BLK), head * HEAD_DIM:(head + 1) * HEAD_DIM] = out
        return state

    state = tuple((c0_ref[head], n0_ref[head], m0_ref[head]) for head in range(n_heads))
    state = lax.fori_loop(0, nchunk - 1, body, state)
    valid = (tcol + last_r0 >= done) & (tcol + last_r0 < t)
    outs, state = chunk(last_r0, valid, gs[last_r0:last_r0 + BLK, :], state)
    for head, out in enumerate(outs):
        h_ref[done:t, head * HEAD_DIM:(head + 1) * HEAD_DIM] = out[done - last_r0:t - last_r0]
        c_ref[head], n_ref[head], m_ref[head] = state[head]


def _mlstm(p3, g3, b_i_l, b_f_l, g_mix_l, c0, n0, m0, col0, col_mix, gate_lane, out_dtype):
    nb, t, _ = p3.shape
    n_heads = c0.shape[1]
    d_ml = n_heads * HEAD_DIM
    cb = col0 // d_ml
    seq = lambda j: pl.BlockSpec((None, t, d_ml), lambda b: (b, 0, cb + j))
    vec_spec = pl.BlockSpec((None, n_heads, 1, HEAD_DIM), lambda b: (b, 0, 0, 0))
    mat_spec = pl.BlockSpec((None, n_heads, HEAD_DIM, HEAD_DIM), lambda b: (b, 0, 0, 0))
    smem = pl.BlockSpec(memory_space=pltpu.SMEM)
    pads = [pltpu.VMEM((BLK, d_ml), F32)] * 4 + [pltpu.VMEM((BLK, LANES), F32)] if t < BLK else []
    return pl.pallas_call(
        functools.partial(_mlstm_kernel, gate_lane=gate_lane),
        grid=(nb,),
        in_specs=[smem, smem, seq(0), seq(1), seq(2), seq(3),
                  pl.BlockSpec((None, t, LANES), lambda b: (b, 0, 0)),
                  pl.BlockSpec((1, d_ml), lambda b: (0, col_mix // d_ml)),
                  mat_spec, vec_spec, vec_spec],
        out_specs=[pl.BlockSpec((None, t, d_ml), lambda b: (b, 0, 0)),
                   mat_spec, vec_spec, vec_spec],
        out_shape=[jax.ShapeDtypeStruct((nb, t, d_ml), out_dtype),
                   jax.ShapeDtypeStruct(c0.shape, F32),
                   jax.ShapeDtypeStruct(n0.shape, F32),
                   jax.ShapeDtypeStruct(m0.shape, F32)],
        scratch_shapes=pads,
        compiler_params=_cparams(1),
        name="mlstm",
    )(b_i_l, b_f_l, p3, p3, p3, p3, g3, g_mix_l, c0, n0, m0)


def _merge_rows(ha, yc, hm, x, w_ref, g_ref, b_ref, alpha):
    r = alpha * x
    k0 = 0
    for src in (ha, yc, hm):
        k1 = k0 + src.shape[1]
        r = r + jnp.dot(src.astype(BF16), w_ref[k0:k1, :], preferred_element_type=F32)
        k0 = k1
    mu = jnp.mean(r, axis=-1, keepdims=True)
    d = r - mu
    var = jnp.mean(d * d, axis=-1, keepdims=True)
    return d * lax.rsqrt(var + LN_EPS) * g_ref[...] + b_ref[...]


def _merge_kernel(ha_ref, yc_ref, hm_ref, x_ref, has_ref, ycs_ref, hms_ref, xs_ref, w_ref, g_ref, b_ref,
                  o_ref, os_ref, *, alpha):
    o_ref[...] = _merge_rows(ha_ref[...], yc_ref[...], hm_ref[...], x_ref[...], w_ref, g_ref, b_ref,
                             alpha)

    @pl.when(pl.program_id(0) == 0)
    def _():
        os_ref[...] = _merge_rows(has_ref[...], ycs_ref[...], hms_ref[...], xs_ref[...], w_ref, g_ref,
                                  b_ref, alpha)


def _merge(heads, x, heads_s, xs, w_out, layer, ln_g, ln_b, alpha):
    m, d = x.shape
    ms = xs.shape[0]
    d_mix = w_out.shape[1]
    tm = _row_tile(m)
    rows = lambda a: pl.BlockSpec((tm, a.shape[1]), lambda i: (i, 0))
    whole = lambda a: pl.BlockSpec(a.shape, lambda i: (0, 0))
    return pl.pallas_call(
        functools.partial(_merge_kernel, alpha=alpha),
        grid=(m // tm,),
        in_specs=[rows(a) for a in heads] + [rows(x)] + [whole(a) for a in heads_s] + [whole(xs)]
                 + [pl.BlockSpec((None, d_mix, d), lambda i: (layer, 0, 0), pipeline_mode=pl.Buffered(1)),
                    whole(ln_g), whole(ln_b)],
        out_specs=[pl.BlockSpec((tm, d), lambda i: (i, 0)), pl.BlockSpec((ms, d), lambda i: (0, 0))],
        out_shape=[jax.ShapeDtypeStruct((m, d), F32), jax.ShapeDtypeStruct((ms, d), F32)],
        compiler_params=_cparams(1),
        name="merge",
    )(*heads, x, *heads_s, xs, w_out, ln_g, ln_b)


def _mlp_part(xb, wu_ref, wd_ref):
    u = jnp.maximum(jnp.dot(xb, wu_ref[...], preferred_element_type=F32), 0.0)
    return jnp.dot((u * u).astype(BF16), wd_ref[...], preferred_element_type=F32)


def _ln_rows(r, g_ref, b_ref):
    mu = jnp.mean(r, axis=-1, keepdims=True)
    d = r - mu
    var = jnp.mean(d * d, axis=-1, keepdims=True)
    return d * lax.rsqrt(var + LN_EPS) * g_ref[...] + b_ref[...]


def _mlp_kernel(x_ref, xs_ref, wu_ref, wd_ref, g_ref, b_ref, o_ref, os_ref, xb_ref, acc_ref, accs_ref,
                *, alpha):
    i, j = pl.program_id(0), pl.program_id(1)
    last = pl.num_programs(1) - 1

    @pl.when(j == 0)
    def _():
        x = x_ref[...]
        xb_ref[...] = x.astype(BF16)
        acc_ref[...] = alpha * x

    acc_ref[...] += _mlp_part(xb_ref[...], wu_ref, wd_ref)

    @pl.when(j == last)
    def _():
        o_ref[...] = _ln_rows(acc_ref[...], g_ref, b_ref)

    @pl.when(i == 0)
    def _():
        xs = xs_ref[...]
        part = _mlp_part(xs.astype(BF16), wu_ref, wd_ref)

        @pl.when(j == 0)
        def _():
            accs_ref[...] = alpha * xs + part

        @pl.when(j > 0)
        def _():
            accs_ref[...] += part

        @pl.when(j == last)
        def _():
            os_ref[...] = _ln_rows(accs_ref[...], g_ref, b_ref)


def _mlp(x, xs, w_up, w_down, layer, ln_g, ln_b, alpha, tf):
    m, d = x.shape
    ms = xs.shape[0]
    d_ff = w_up.shape[2]
    tm = _row_tile(m)
    vec = pl.BlockSpec((1, d), lambda i, j: (0, 0))
    return pl.pallas_call(
        functools.partial(_mlp_kernel, alpha=alpha),
        grid=(m // tm, d_ff // tf),
        in_specs=[pl.BlockSpec((tm, d), lambda i, j: (i, 0)),
                  pl.BlockSpec((ms, d), lambda i, j: (0, 0)),
                  pl.BlockSpec((None, d, tf), lambda i, j: (layer, 0, j)),
                  pl.BlockSpec((None, tf, d), lambda i, j: (layer, j, 0)),
                  vec, vec],
        out_specs=[pl.BlockSpec((tm, d), lambda i, j: (i, 0)),
                   pl.BlockSpec((ms, d), lambda i, j: (0, 0))],
        out_shape=[jax.ShapeDtypeStruct((m, d), F32), jax.ShapeDtypeStruct((ms, d), F32)],
        scratch_shapes=[pltpu.VMEM((tm, d), BF16), pltpu.VMEM((tm, d), F32), pltpu.VMEM((ms, d), F32)],
        compiler_params=_cparams(2),
        name="mlp",
    )(x, xs, w_up, w_down, ln_g, ln_b)


def kernel(x_prompt, x_sample, cache_k, cache_v, page_table, state_conv, state_C, state_n, state_m,
           meta_tokens, w_in, sb_bias, conv_w, b_i, b_f, g_mix, w_out, ln1_g, ln1_b, w_up, w_down,
           ln2_g, ln2_b):
    depth, d_model, d_in = w_in.shape
    bp, n_seq, _ = x_prompt.shape
    bs, n_dec, _ = x_sample.shape
    n_phys, page, h_att = cache_k.shape[1:4]
    assert page == PAGE_SIZE and n_dec < PAGE_SIZE
    h_ml = b_i.shape[1]
    d_att = h_att * HEAD_DIM
    d_conv = conv_w.shape[2]
    d_ml = h_ml * HEAD_DIM
    d_main = 3 * d_att + 3 * d_conv + 4 * d_ml
    assert d_in == d_main + 2 * h_ml and 2 * h_ml <= LANES
    t_p = N_META + n_seq
    alpha = (2 * depth) ** 0.25
    col_conv = 3 * d_att
    col_ml = col_conv + 3 * d_conv
    tn_proj = d_main // 4
    tf = 512

    w_in_b = jnp.swapaxes(w_in, 1, 2).astype(BF16)
    w_gate = w_in_b[:, d_in - LANES:, :]
    gate_lane = LANES - 2 * h_ml
    w_out_b = w_out.astype(BF16)
    w_up_b = w_up.astype(BF16)
    w_down_b = w_down.astype(BF16)

    pos = jnp.arange(BLK)
    u_strict = (pos[:, None] > pos[None, :]).astype(BF16)
    u_half = jnp.concatenate([u_strict, jnp.ones((BLK, LANES), BF16)], axis=1)
    u2 = jnp.concatenate([u_half, u_half], axis=0)

    row2 = lambda a: a.reshape(1, -1)
    lane_bcast = lambda a: jnp.broadcast_to(a[..., None, None], a.shape + (1, LANES))

    xp = jnp.concatenate([jnp.broadcast_to(meta_tokens[None], (bp, N_META, d_model)), x_prompt],
                         axis=1).reshape(bp * t_p, d_model)
    xs = x_sample.reshape(bs * n_dec, d_model)

    zero_prev = jnp.zeros((bp, SUBLANES, d_conv), F32)
    zero_c = jnp.zeros((bp, h_ml, HEAD_DIM, HEAD_DIM), F32)
    zero_v = jnp.zeros((bp, h_ml, 1, LANES), F32)

    kv_prompt = None
    outs = {name: [] for name in ("cp", "Cp", "np", "mp", "ks", "vs", "cs", "Cs", "ns", "ms")}

    for l in range(depth):
        gm = row2(g_mix[l])
        p, g, p_s, g_s = _proj(xp, xs, w_in_b, w_gate, l, d_main, tn_proj)

        p3 = p.reshape(bp, t_p, d_main)
        g3 = g.reshape(bp, t_p, LANES)
        h_a, k_all, v_all = _attn_prompt(p3, sb_bias[l], u2, gm, kv_prompt, l, depth, N_META)
        kv_prompt = (k_all, v_all)
        y_c, conv_st = _conv(p3, zero_prev, conv_w[l], gm, col_conv, d_att, d_conv, BF16)
        h_m, c_fin, n_fin, m_fin = _mlstm(p3, g3, b_i[l], b_f[l], gm, zero_c, zero_v, zero_v, col_ml,
                                          d_att + d_conv, gate_lane, BF16)
        outs["cp"].append(conv_st)
        outs["Cp"].append(c_fin)
        outs["np"].append(n_fin[:, :, 0, :])
        outs["mp"].append(m_fin[:, :, 0, 0])
        heads_p = (h_a.reshape(bp * t_p, d_att), y_c.reshape(bp * t_p, d_conv),
                   h_m.reshape(bp * t_p, d_ml))

        p3 = p_s.reshape(bs, n_dec, d_main)
        g3 = g_s.reshape(bs, n_dec, LANES)
        q_s = p3[:, :, :d_att].reshape(bs, n_dec, h_att, HEAD_DIM)
        k_s = p3[:, :, d_att:2 * d_att].reshape(bs, n_dec, h_att, HEAD_DIM)
        v_s = p3[:, :, 2 * d_att:3 * d_att].reshape(bs, n_dec, h_att, HEAD_DIM)
        h_a = _attn_sample(q_s, k_s, v_s, cache_k, cache_v, page_table, sb_bias[l], u2, gm, l)
        prev8 = jnp.pad(state_conv[l], ((0, 0), (SUBLANES - (CONV_WIDTH - 1), 0), (0, 0)))
        y_c, conv_st = _conv(p3, prev8, conv_w[l], gm, col_conv, d_att, d_conv, F32)
        h_m, c_fin, n_fin, m_fin = _mlstm(p3, g3, b_i[l], b_f[l], gm, state_C[l],
                                          state_n[l][:, :, None, :], lane_bcast(state_m[l]), col_ml,
                                          d_att + d_conv, gate_lane, F32)
        outs["ks"].append(k_s)
        outs["vs"].append(v_s)
        outs["cs"].append(conv_st)
        outs["Cs"].append(c_fin)
        outs["ns"].append(n_fin[:, :, 0, :])
        outs["ms"].append(m_fin[:, :, 0, 0])
        heads_s = (h_a.reshape(bs * n_dec, d_att), y_c.reshape(bs * n_dec, d_conv),
                   h_m.reshape(bs * n_dec, d_ml))

        xp, xs = _merge(heads_p, xp, heads_s, xs, w_out_b, l, row2(ln1_g[l]), row2(ln1_b[l]), alpha)
        xp, xs = _mlp(xp, xs, w_up_b, w_down_b, l, row2(ln2_g[l]), row2(ln2_b[l]), alpha, tf)

    y_prompt = xp.reshape(bp, t_p, d_model)[:, N_META:]
    y_sample = xs.reshape(bs, n_dec, d_model)
    k_prompt = kv_prompt[0].reshape(depth, bp, t_p, h_att, HEAD_DIM)
    v_prompt = kv_prompt[1].reshape(depth, bp, t_p, h_att, HEAD_DIM)
    st = lambda name: jnp.stack(outs[name])
    return (y_prompt, y_sample, k_prompt, v_prompt, st("cp"), st("Cp"), st("np"), st("mp"),
            st("ks"), st("vs"), st("cs"), st("Cs"), st("ns"), st("ms"))
```

```python
import functools

import jax
import jax.numpy as jnp
from jax import lax
from jax.experimental import pallas as pl
from jax.experimental.pallas import tpu as pltpu

F32 = jnp.float32
BF16 = jnp.bfloat16

HEAD_DIM = 128
N_META = 16
PAGE_SIZE = 128
CONV_WIDTH = 3
LN_EPS = 1e-5
RMS_EPS = 1e-6
SB_SCALE = HEAD_DIM ** -0.5
MLSTM_K_SCALE = HEAD_DIM ** -0.5
NEG_BIG = -1e30

V7X_VMEM_LIMIT_BYTES = 56 * 1024 * 1024
LANES = 128
SUBLANES = 8
BLK = 128


def _cparams(n_axes):
    return pltpu.CompilerParams(dimension_semantics=("arbitrary",) * n_axes,
                                vmem_limit_bytes=V7X_VMEM_LIMIT_BYTES)


def _row_tile(m, cap=704):
    if m <= cap:
        return m
    best = None
    for t in range(16, cap + 1, 16):
        if m % t == 0:
            best = t
    assert best is not None, m
    return best


def _log_sigmoid_pair(z):
    sp = jnp.log(1.0 + jnp.exp(-jnp.abs(z)))
    return jnp.minimum(z, 0.0) - sp, -jnp.maximum(z, 0.0) - sp


def _head_rms(x, g):
    ms = jnp.mean(x * x, axis=-1, keepdims=True)
    return x * lax.rsqrt(ms + RMS_EPS) * g


def _qk(qt, kt):
    return lax.dot_general(qt, kt, (((1,), (1,)), ((), ())), preferred_element_type=F32)


def _split_bf16(x):
    hi = x.astype(BF16)
    return hi, (x - hi.astype(F32)).astype(BF16)


def _proj_kernel(x_ref, xs_ref, w_ref, wg_ref, p_ref, g_ref, ps_ref, gs_ref, xb_ref):
    i, j = pl.program_id(0), pl.program_id(1)

    @pl.when(j == 0)
    def _():
        xb = x_ref[...].astype(BF16)
        xb_ref[...] = xb
        g_ref[...] = _qk(xb, wg_ref[...])

    p_ref[...] = _qk(xb_ref[...], w_ref[...])

    @pl.when(i == 0)
    def _():
        xsb = xs_ref[...].astype(BF16)
        ps_ref[...] = _qk(xsb, w_ref[...])

        @pl.when(j == 0)
        def _():
            gs_ref[...] = _qk(xsb, wg_ref[...])


def _proj(x, xs, w, wg, layer, n, tn):
    m, k = x.shape
    ms = xs.shape[0]
    tm = _row_tile(m)
    n_col = n // tn
    return pl.pallas_call(
        _proj_kernel,
        grid=(m // tm, n_col),
        in_specs=[pl.BlockSpec((tm, k), lambda i, j: (i, 0)),
                  pl.BlockSpec((ms, k), lambda i, j: (0, 0)),
                  pl.BlockSpec((None, tn, k), lambda i, j: (layer, j, 0)),
                  pl.BlockSpec((None, LANES, k), lambda i, j: (layer, 0, 0))],
        out_specs=[pl.BlockSpec((tm, tn), lambda i, j: (i, j)),
                   pl.BlockSpec((tm, LANES), lambda i, j: (i, 0)),
                   pl.BlockSpec((ms, tn), lambda i, j: (0, jnp.where(i == 0, j, n_col - 1))),
                   pl.BlockSpec((ms, LANES), lambda i, j: (0, 0))],
        out_shape=[jax.ShapeDtypeStruct((m, n), F32),
                   jax.ShapeDtypeStruct((m, LANES), F32),
                   jax.ShapeDtypeStruct((ms, n), F32),
                   jax.ShapeDtypeStruct((ms, LANES), F32)],
        scratch_shapes=[pltpu.VMEM((tm, k), BF16)],
        compiler_params=_cparams(2),
        name="proj",
    )(x, xs, w, wg)


def _sb_scores(s, bias, u2, mask):
    tq, nk = s.shape
    m = nk // BLK
    lsz, lk = _log_sigmoid_pair(s * SB_SCALE + bias)
    if mask is not None:
        lk = jnp.where(mask, lk, 0.0)
    hi, lo = _split_bf16(lk)
    lanes = lambda x, j: x[:, j * BLK:(j + 1) * BLK]
    stacked = jnp.concatenate(
        [jnp.concatenate([lanes(hi, j), lanes(lo, j)], axis=1) for j in range(m)], axis=0)
    cs = jnp.dot(stacked, u2, preferred_element_type=F32)
    pieces, later = [None] * m, None
    for j in range(m - 1, -1, -1):
        sfx, tot = cs[j * tq:(j + 1) * tq, :BLK], cs[j * tq:(j + 1) * tq, BLK:]
        pieces[j] = lanes(lsz, j) + (sfx if later is None else sfx + later)
        later = tot if later is None else later + tot
    return (pieces[0] if m == 1 else jnp.concatenate(pieces, axis=1)), later


def _sb_weighted(logit, carry, vt, mask):
    if carry is not None:
        m = logit.shape[1] // BLK
        logit = logit + (carry if m == 1 else jnp.concatenate([carry] * m, axis=1))
    a = jnp.exp(logit)
    if mask is not None:
        a = jnp.where(mask, a, 0.0)
    return jnp.dot(a.astype(BF16), vt, preferred_element_type=F32)


def _attn_kernel(bias_ref, q_ref, k_ref, v_ref, u2_ref, gmix_ref, o_ref, ko_ref, vo_ref,
                 qb, kb, vb, carry_ref, acc_ref, *s_refs, n_lead, run):
    t = q_ref.shape[0]
    tq = carry_ref.shape[0]
    n_run = tq // run
    n_qblk = (t - n_lead) // tq
    bias = bias_ref[pl.program_id(1)]

    qb[...] = q_ref[...].astype(BF16)
    kb[...] = k_ref[...].astype(BF16)
    vb[...] = v_ref[...].astype(BF16)
    ko_ref[...] = k_ref[...]
    vo_ref[...] = v_ref[...]
    u2 = u2_ref[...]

    row = lax.broadcasted_iota(jnp.int32, (BLK, BLK), 0)
    col = lax.broadcasted_iota(jnp.int32, (BLK, BLK), 1)
    logit, _ = _sb_scores(_qk(qb[0:BLK, :], kb[0:BLK, :]), bias, u2, col < row)
    lead = _sb_weighted(logit, None, vb[0:BLK, :], col < row)[0:n_lead]
    o_ref[0:n_lead, :] = _head_rms(lead, gmix_ref[...]).astype(o_ref.dtype)

    def q_block(i, _):
        r0 = pl.multiple_of(n_lead + i * tq, 16)
        qt = qb[pl.ds(r0, tq), :]

        def products(c0):
            return _qk(qt, kb[pl.ds(pl.multiple_of(jnp.maximum(c0, n_lead), 16), run), :])

        def step(s, c0):
            logit, tot = _sb_scores(s, bias, u2, None)
            carry = carry_ref[...]
            acc_ref[...] += _sb_weighted(logit, carry, vb[pl.ds(pl.multiple_of(c0, 16), run), :], None)
            carry_ref[...] = carry + tot

        carry = acc = None
        for d in range(n_run - 1, -1, -1):
            c0 = pl.multiple_of(r0 + d * run, 16)
            mask = (lax.broadcasted_iota(jnp.int32, (tq, run), 1) + d * run
                    < lax.broadcasted_iota(jnp.int32, (tq, run), 0))
            logit, tot = _sb_scores(_qk(qt, kb[pl.ds(c0, run), :]), bias, u2, mask)
            pv = _sb_weighted(logit, carry, vb[pl.ds(c0, run), :], mask)
            acc = pv if acc is None else acc + pv
            carry = tot if carry is None else carry + tot
        carry_ref[...] = carry
        acc_ref[...] = acc

        if n_run % 2 == 0:
            s_even, s_odd = s_refs
            s_even[...] = products(r0 - run)

            def pair(n, _):
                c0 = r0 - (2 * n + 1) * run
                s = s_even[...]
                s_odd[...] = products(c0 - run)
                step(s, c0)
                s = s_odd[...]
                s_even[...] = products(c0 - 2 * run)
                step(s, c0 - run)
                return 0

            lax.fori_loop(0, i * (n_run // 2), pair, 0)
        else:
            def single(n, _):
                c0 = r0 - (n + 1) * run
                step(products(c0), c0)
                return 0

            lax.fori_loop(0, i * n_run, single, 0)

        mask = lax.broadcasted_iota(jnp.int32, (tq, BLK), 1) < n_lead
        logit, _ = _sb_scores(_qk(qt, kb[0:BLK, :]), bias, u2, mask)
        out = acc_ref[...] + _sb_weighted(logit, carry_ref[...], vb[0:BLK, :], mask)
        o_ref[pl.ds(r0, tq), :] = _head_rms(out, gmix_ref[...]).astype(o_ref.dtype)
        return 0

    lax.fori_loop(0, n_qblk, q_block, 0)


def _attn_kernel_aliased(bias_ref, q_ref, k_ref, v_ref, u2_ref, gmix_ref, kprev_ref, vprev_ref,
                         *rest, **kw):
    del kprev_ref, vprev_ref
    _attn_kernel(bias_ref, q_ref, k_ref, v_ref, u2_ref, gmix_ref, *rest, **kw)


def _attn_prompt(p3, sb_bias_l, u2, g_mix_l, kv_prev, layer, depth, n_lead):
    nb, t, _ = p3.shape
    n_heads = sb_bias_l.shape[0]
    assert 0 < n_lead <= BLK <= t and n_lead % 16 == 0
    tq = max(c for c in (512, 256, 128) if (t - n_lead) % c == 0)
    run = min(tq, 256)
    seq = lambda off: pl.BlockSpec((None, t, HEAD_DIM), lambda b, h: (b, 0, off + h))
    kv_spec = pl.BlockSpec((None, None, t, HEAD_DIM), lambda b, h: (layer, b, 0, h))
    kv_shape = jax.ShapeDtypeStruct((depth, nb, t, n_heads * HEAD_DIM), F32)
    in_specs = [pl.BlockSpec(memory_space=pltpu.SMEM), seq(0), seq(n_heads), seq(2 * n_heads),
                pl.BlockSpec((2 * BLK, BLK + LANES), lambda b, h: (0, 0)),
                pl.BlockSpec((1, HEAD_DIM), lambda b, h: (0, h))]
    args = [sb_bias_l, p3, p3, p3, u2, g_mix_l]
    aliases = {}
    body = _attn_kernel
    if kv_prev is not None:
        in_specs += [pl.BlockSpec(memory_space=pl.ANY)] * 2
        args += list(kv_prev)
        aliases = {6: 1, 7: 2}
        body = _attn_kernel_aliased
    return pl.pallas_call(
        functools.partial(body, n_lead=n_lead, run=run),
        grid=(nb, n_heads),
        in_specs=in_specs,
        out_specs=[pl.BlockSpec((None, t, HEAD_DIM), lambda b, h: (b, 0, h)), kv_spec, kv_spec],
        out_shape=[jax.ShapeDtypeStruct((nb, t, n_heads * HEAD_DIM), BF16), kv_shape, kv_shape],
        scratch_shapes=[pltpu.VMEM((t, HEAD_DIM), BF16)] * 3 + [pltpu.VMEM((tq, LANES), F32)] * 2
                       + [pltpu.VMEM((tq, run), F32)] * 2,
        input_output_aliases=aliases,
        compiler_params=_cparams(2),
        name="attn_prompt",
    )(*args)


def _attn_sample_kernel(pt_ref, q_ref, bias_ref, own_ref, new_ref, u2_ref, grow_ref, kn_ref, vn_ref,
                        *refs, n_group):
    del pt_ref
    kc_refs, vc_refs = refs[:n_group], refs[n_group:2 * n_group]
    o_ref, acc_ref, carry_ref = refs[2 * n_group:]
    p = pl.program_id(1)
    n_rows = acc_ref.shape[0]
    n_lanes = own_ref.shape[1]
    n_blk = n_lanes // LANES

    def scores(k_ref, keep):
        k2 = k_ref[...].reshape(n_lanes, HEAD_DIM).astype(BF16)
        s = lax.dot_general(q_ref[...], k2, (((1,), (1,)), ((), ())), preferred_element_type=F32)
        lsz, lk = _log_sigmoid_pair(s * SB_SCALE + bias_ref[...])
        hi, lo = _split_bf16(lk * keep)
        lane_blocks = lambda x: jnp.concatenate(
            [x[:, j * LANES:(j + 1) * LANES] for j in range(n_blk)], axis=0)
        return lsz, jnp.concatenate([lane_blocks(hi), lane_blocks(lo)], axis=1)

    def weighted(lsz, cs, v_ref, keep, acc, carry):
        logits = [None] * n_blk
        for j in range(n_blk - 1, -1, -1):
            rows = slice(j * n_rows, (j + 1) * n_rows)
            logits[j] = cs[rows, :LANES] + carry
            carry = carry + cs[rows, LANES:]
        a = jnp.exp(lsz + jnp.concatenate(logits, axis=1)) * keep
        v2 = v_ref[...].reshape(n_lanes, HEAD_DIM).astype(BF16)
        return acc + jnp.dot(a.astype(BF16), v2, preferred_element_type=F32), carry

    def pages(k_refs, v_refs, keep, acc, carry):
        parts = [scores(k_ref, keep) for k_ref in k_refs]
        cs = jnp.dot(jnp.concatenate([x for _, x in parts], axis=0), u2_ref[...],
                     preferred_element_type=F32)
        per_page = n_blk * n_rows
        for g, v_ref in enumerate(v_refs):
            acc, carry = weighted(parts[g][0], cs[g * per_page:(g + 1) * per_page], v_ref, keep,
                                  acc, carry)
        return acc, carry

    @pl.when(p == 0)
    def _():
        acc, carry = pages([kn_ref], [vn_ref], new_ref[...], jnp.zeros(acc_ref.shape, F32),
                           jnp.zeros(carry_ref.shape, F32))
        acc_ref[...] = acc
        carry_ref[...] = carry

    @pl.when(p > 0)
    def _():
        acc, carry = pages(kc_refs, vc_refs, own_ref[...], acc_ref[...], carry_ref[...])
        acc_ref[...] = acc
        carry_ref[...] = carry

    @pl.when(p == pl.num_programs(1) - 1)
    def _():
        o_ref[...] = _head_rms(acc_ref[...], grow_ref[...])


def _attn_sample(q, k_new, v_new, cache_k, cache_v, page_table, sb_bias_l, u2, g_mix_l, layer):
    nb, n_dec, n_heads, _ = q.shape
    n_pages = page_table.shape[1]
    n_rows = n_heads * n_dec
    n_lanes = PAGE_SIZE * n_heads
    assert n_rows % 16 == 0 and n_heads == SUBLANES
    n_group = max(c for c in (16, 8, 4, 2, 1) if n_pages % c == 0)
    q_rows = jnp.transpose(q, (0, 2, 1, 3)).reshape(nb, n_rows, HEAD_DIM).astype(BF16)
    row_head = jnp.repeat(jnp.arange(n_heads), n_dec)[:, None]
    row_query = jnp.tile(jnp.arange(n_dec), n_heads)[:, None]
    lane_pos = jnp.repeat(jnp.arange(PAGE_SIZE), n_heads)[None, :]
    lane_head = jnp.tile(jnp.arange(n_heads), PAGE_SIZE)[None, :]
    own = (row_head == lane_head).astype(F32)
    own_new = own * (lane_pos < row_query).astype(F32)
    bias_rows = jnp.broadcast_to(jnp.repeat(sb_bias_l, n_dec)[:, None], (n_rows, n_lanes))
    pad_pos = lambda a: jnp.pad(a, ((0, 0), (0, PAGE_SIZE - n_dec), (0, 0), (0, 0)))
    g_rows = jnp.repeat(g_mix_l[0, :n_heads * HEAD_DIM].reshape(n_heads, HEAD_DIM), n_dec, axis=0)

    def cache_spec(g):
        def index_map(b, p, pt):
            logical = n_pages - 1 - (jnp.maximum(p - 1, 0) * n_group + g)
            return (layer, pt[b, logical], 0, 0, 0)
        return pl.BlockSpec((None, None, PAGE_SIZE, n_heads, HEAD_DIM), index_map)

    new_spec = pl.BlockSpec((None, PAGE_SIZE, n_heads, HEAD_DIM), lambda b, p, pt: (b, 0, 0, 0))
    row_const = pl.BlockSpec((n_rows, n_lanes), lambda b, p, pt: (0, 0))
    cache_specs = [cache_spec(g) for g in range(n_group)]
    out = pl.pallas_call(
        functools.partial(_attn_sample_kernel, n_group=n_group),
        grid_spec=pltpu.PrefetchScalarGridSpec(
            num_scalar_prefetch=1,
            grid=(nb, n_pages // n_group + 1),
            in_specs=[pl.BlockSpec((None, n_rows, HEAD_DIM), lambda b, p, pt: (b, 0, 0)),
                      row_const, row_const, row_const,
                      pl.BlockSpec(u2.shape, lambda b, p, pt: (0, 0)),
                      pl.BlockSpec((n_rows, HEAD_DIM), lambda b, p, pt: (0, 0)),
                      new_spec, new_spec] + cache_specs + cache_specs,
            out_specs=pl.BlockSpec((None, n_rows, HEAD_DIM), lambda b, p, pt: (b, 0, 0)),
            scratch_shapes=[pltpu.VMEM((n_rows, HEAD_DIM), F32), pltpu.VMEM((n_rows, LANES), F32)]),
        out_shape=jax.ShapeDtypeStruct((nb, n_rows, HEAD_DIM), F32),
        compiler_params=_cparams(2),
        name="attn_sample",
    )(page_table, q_rows, bias_rows, own, own_new, u2, g_rows, pad_pos(k_new), pad_pos(v_new),
      *([cache_k] * n_group), *([cache_v] * n_group))
    return jnp.transpose(out.reshape(nb, n_heads, n_dec, HEAD_DIM), (0, 2, 1, 3)).reshape(
        nb, n_dec, n_heads * HEAD_DIM)


def _conv_kernel(bc_ref, cc_ref, xc_ref, prev_ref, w_ref, gmix_ref, y_ref, st_ref, ext_ref):
    t = bc_ref.shape[0]
    te = ext_ref.shape[0]
    u = cc_ref[...] * xc_ref[...]
    ext_ref[0:SUBLANES, :] = prev_ref[...]
    ext_ref[te - SUBLANES:te, :] = jnp.zeros((SUBLANES, u.shape[1]), F32)
    ext_ref[SUBLANES:SUBLANES + t, :] = u
    e0 = ext_ref[...]
    e1 = pltpu.roll(e0, 1, 0)
    e2 = pltpu.roll(e0, 2, 0)
    w = w_ref[...]
    lo, hi = SUBLANES, SUBLANES + t
    y = bc_ref[...] * (w[0:1] * e2[lo:hi] + w[1:2] * e1[lo:hi] + w[2:3] * e0[lo:hi])
    for s in range(y.shape[1] // HEAD_DIM):
        sl = slice(s * HEAD_DIM, (s + 1) * HEAD_DIM)
        y_ref[:, sl] = _head_rms(y[:, sl], gmix_ref[:, sl]).astype(y_ref.dtype)
    st_ref[...] = ext_ref[hi - (CONV_WIDTH - 1):hi, :]


def _conv(p3, prev8, conv_w_l, g_mix_l, col0, col_mix, d_conv, out_dtype):
    nb, t, _ = p3.shape
    cb = col0 // d_conv
    te = SUBLANES + pl.cdiv(t, SUBLANES) * SUBLANES
    seq = lambda j: pl.BlockSpec((None, t, d_conv), lambda b: (b, 0, cb + j))
    return pl.pallas_call(
        _conv_kernel,
        grid=(nb,),
        in_specs=[seq(0), seq(1), seq(2),
                  pl.BlockSpec((None, SUBLANES, d_conv), lambda b: (b, 0, 0)),
                  pl.BlockSpec((CONV_WIDTH, d_conv), lambda b: (0, 0)),
                  pl.BlockSpec((1, d_conv), lambda b: (0, col_mix // d_conv))],
        out_specs=[pl.BlockSpec((None, t, d_conv), lambda b: (b, 0, 0)),
                   pl.BlockSpec((None, CONV_WIDTH - 1, d_conv), lambda b: (b, 0, 0))],
        out_shape=[jax.ShapeDtypeStruct((nb, t, d_conv), out_dtype),
                   jax.ShapeDtypeStruct((nb, CONV_WIDTH - 1, d_conv), F32)],
        scratch_shapes=[pltpu.VMEM((te, d_conv), F32)],
        compiler_params=_cparams(1),
        name="conv",
    )(p3, p3, p3, prev8, conv_w_l, g_mix_l)


def _mlstm_kernel(bi_ref, bf_ref, q_ref, k_ref, v_ref, om_ref, g_ref, gmix_ref, c0_ref, n0_ref,
                  m0_ref, h_ref, c_ref, n_ref, m_ref, *pad_refs, gate_lane):
    t = q_ref.shape[0]
    n_heads = c0_ref.shape[0]
    nchunk = pl.cdiv(t, BLK)
    done = (nchunk - 1) * BLK
    srcs = (q_ref, k_ref, v_ref, om_ref, g_ref)
    if t < BLK:
        for src, dst in zip(srcs, pad_refs):
            dst[...] = jnp.zeros(dst.shape, F32)
            dst[0:t, :] = src[...]
        srcs = pad_refs
    qs, ks, vs, oms, gs = srcs
    last_r0 = max(t - BLK, 0)

    row = lax.broadcasted_iota(jnp.int32, (BLK, BLK), 0)
    col = lax.broadcasted_iota(jnp.int32, (BLK, BLK), 1)
    eye = row == col
    causal = col <= row
    tcol = lax.broadcasted_iota(jnp.int32, (BLK, 1), 0)

    def gates(head, r0, valid, g):
        lanes = slice(head * HEAD_DIM, (head + 1) * HEAD_DIM)
        q = qs[pl.ds(r0, BLK), lanes].astype(BF16)
        k = (ks[pl.ds(r0, BLK), lanes] * MLSTM_K_SCALE).astype(BF16)
        im_col = jnp.sum(jnp.where(col == gate_lane + head, g, 0.0), axis=1, keepdims=True)
        fm_col = jnp.sum(jnp.where(col == gate_lane + n_heads + head, g, 0.0), axis=1, keepdims=True)
        ig_col = im_col + bi_ref[head]
        lf_col = _log_sigmoid_pair(fm_col + bf_ref[head])[0]
        if valid is not None:
            ig_col = jnp.where(valid, ig_col, NEG_BIG)
            lf_col = jnp.where(valid, lf_col, 0.0)
        ig_row = jnp.sum(jnp.where(eye, ig_col, 0.0), axis=0, keepdims=True)
        lf_row = jnp.sum(jnp.where(eye, lf_col, 0.0), axis=0, keepdims=True)
        cf_col = jnp.sum(jnp.where(causal, lf_row, 0.0), axis=1, keepdims=True)
        cf_row = jnp.sum(jnp.where(row <= col, lf_col, 0.0), axis=0, keepdims=True)
        log_d = jnp.where(causal, cf_col - cf_row + ig_row, -jnp.inf)
        cf_last = jnp.sum(lf_row, axis=1, keepdims=True)
        log_w = cf_last - cf_row + ig_row
        return dict(q=q, k=k, vf=vs[pl.ds(r0, BLK), lanes], om=oms[pl.ds(r0, BLK), lanes],
                    log_d=log_d, max_d=jnp.max(log_d, axis=1, keepdims=True), qk=_qk(q, k),
                    cf_col=cf_col, cf_last=cf_last, log_w=log_w,
                    max_w=jnp.max(log_w, axis=1, keepdims=True), lanes=lanes)

    def advance(pre, c0, n0, m0):
        q, k, vf = pre["q"], pre["k"], pre["vf"]
        m0 = m0[:, 0:1]
        log_inter = pre["cf_col"] + m0
        m_t = jnp.maximum(log_inter, pre["max_d"])
        w = jnp.exp(pre["log_d"] - m_t) * pre["qk"]
        s_inter = jnp.exp(log_inter - m_t)
        num = s_inter * _qk(q, c0.astype(BF16)) + jnp.dot(w.astype(BF16), vf.astype(BF16),
                                                           preferred_element_type=F32)
        qn = jnp.sum(q.astype(F32) * n0, axis=1, keepdims=True)
        den = s_inter * qn + jnp.sum(w, axis=1, keepdims=True)
        h = num / jnp.maximum(jnp.abs(den), jnp.exp(-m_t))
        gated = jax.nn.sigmoid(pre["om"]) * h

        m_new = jnp.maximum(pre["cf_last"] + m0, pre["max_w"])
        wk_row = jnp.exp(pre["log_w"] - m_new)
        decay = jnp.exp(pre["cf_last"] + m0 - m_new)
        wk_col = jnp.sum(jnp.where(eye, wk_row, 0.0), axis=1, keepdims=True)
        vs_t = (vf * wk_col).T.astype(BF16)
        c_new = decay * c0 + jnp.dot(vs_t, k, preferred_element_type=F32)
        n_new = decay * n0 + jnp.sum(k.astype(F32) * wk_col, axis=0, keepdims=True)
        out = _head_rms(gated, gmix_ref[:, pre["lanes"]]).astype(h_ref.dtype)
        return out, c_new, n_new, jnp.broadcast_to(m_new, (1, LANES))

    def chunk(r0, valid, g, state):
        pres = [gates(head, r0, valid, g) for head in range(n_heads)]
        results = [advance(pres[head], *state[head]) for head in range(n_heads)]
        return [r[0] for r in results], tuple(r[1:] for r in results)

    def body(ci, state):
        r0 = pl.multiple_of(ci * BLK, BLK)
        outs, state = chunk(r0, None, gs[pl.ds(r0, BLK), :], state)
        for head, out in enumerate(outs):
            h_ref[pl.ds(r0, BLK), head * HEAD_DIM:(head + 1) * HEAD_DIM] = out
        return state

    state = tuple((c0_ref[head], n0_ref[head], m0_ref[head]) for head in range(n_heads))
    state = lax.fori_loop(0, nchunk - 1, body, state)
    valid = (tcol + last_r0 >= done) & (tcol + last_r0 < t)
    outs, state = chunk(last_r0, valid, gs[last_r0:last_r0 + BLK, :], state)
    for head, out in enumerate(outs):
        h_ref[done:t, head * HEAD_DIM:(head + 1) * HEAD_DIM] = out[done - last_r0:t - last_r0]
        c_ref[head], n_ref[head], m_ref[head] = state[head]


def _mlstm(p3, g3, b_i_l, b_f_l, g_mix_l, c0, n0, m0, col0, col_mix, gate_lane, out_dtype):
    nb, t, _ = p3.shape
    n_heads = c0.shape[1]
    d_ml = n_heads * HEAD_DIM
    cb = col0 // d_ml
    seq = lambda j: pl.BlockSpec((None, t, d_ml), lambda b: (b, 0, cb + j))
    vec_spec = pl.BlockSpec((None, n_heads, 1, HEAD_DIM), lambda b: (b, 0, 0, 0))
    mat_spec = pl.BlockSpec((None, n_heads, HEAD_DIM, HEAD_DIM), lambda b: (b, 0, 0, 0))
    smem = pl.BlockSpec(memory_space=pltpu.SMEM)
    pads = [pltpu.VMEM((BLK, d_ml), F32)] * 4 + [pltpu.VMEM((BLK, LANES), F32)] if t < BLK else []
    return pl.pallas_call(
        functools.partial(_mlstm_kernel, gate_lane=gate_lane),
        grid=(nb,),
        in_specs=[smem, smem, seq(0), seq(1), seq(2), seq(3),
                  pl.BlockSpec((None, t, LANES), lambda b: (b, 0, 0)),
                  pl.BlockSpec((1, d_ml), lambda b: (0, col_mix // d_ml)),
                  mat_spec, vec_spec, vec_spec],
        out_specs=[pl.BlockSpec((None, t, d_ml), lambda b: (b, 0, 0)),
                   mat_spec, vec_spec, vec_spec],
        out_shape=[jax.ShapeDtypeStruct((nb, t, d_ml), out_dtype),
                   jax.ShapeDtypeStruct(c0.shape, F32),
                   jax.ShapeDtypeStruct(n0.shape, F32),
                   jax.ShapeDtypeStruct(m0.shape, F32)],
        scratch_shapes=pads,
        compiler_params=_cparams(1),
        name="mlstm",
    )(b_i_l, b_f_l, p3, p3, p3, p3, g3, g_mix_l, c0, n0, m0)


def _merge_rows(ha, yc, hm, x, w_ref, g_ref, b_ref, alpha):
    r = alpha * x
    k0 = 0
    for src in (ha, yc, hm):
        k1 = k0 + src.shape[1]
        r = r + jnp.dot(src.astype(BF16), w_ref[k0:k1, :], preferred_element_type=F32)
        k0 = k1
    mu = jnp.mean(r, axis=-1, keepdims=True)
    d = r - mu
    var = jnp.mean(d * d, axis=-1, keepdims=True)
    return d * lax.rsqrt(var + LN_EPS) * g_ref[...] + b_ref[...]


def _merge_kernel(ha_ref, yc_ref, hm_ref, x_ref, has_ref, ycs_ref, hms_ref, xs_ref, w_ref, g_ref, b_ref,
                  o_ref, os_ref, *, alpha):
    o_ref[...] = _merge_rows(ha_ref[...], yc_ref[...], hm_ref[...], x_ref[...], w_ref, g_ref, b_ref,
                             alpha)

    @pl.when(pl.program_id(0) == 0)
    def _():
        os_ref[...] = _merge_rows(has_ref[...], ycs_ref[...], hms_ref[...], xs_ref[...], w_ref, g_ref,
                                  b_ref, alpha)


def _merge(heads, x, heads_s, xs, w_out, layer, ln_g, ln_b, alpha):
    m, d = x.shape
    ms = xs.shape[0]
    d_mix = w_out.shape[1]
    tm = _row_tile(m)
    rows = lambda a: pl.BlockSpec((tm, a.shape[1]), lambda i: (i, 0))
    whole = lambda a: pl.BlockSpec(a.shape, lambda i: (0, 0))
    return pl.pallas_call(
        functools.partial(_merge_kernel, alpha=alpha),
        grid=(m // tm,),
        in_specs=[rows(a) for a in heads] + [rows(x)] + [whole(a) for a in heads_s] + [whole(xs)]
                 + [pl.BlockSpec((None, d_mix, d), lambda i: (layer, 0, 0), pipeline_mode=pl.Buffered(1)),
                    whole(ln_g), whole(ln_b)],
        out_specs=[pl.BlockSpec((tm, d), lambda i: (i, 0)), pl.BlockSpec((ms, d), lambda i: (0, 0))],
        out_shape=[jax.ShapeDtypeStruct((m, d), F32), jax.ShapeDtypeStruct((ms, d), F32)],
        compiler_params=_cparams(1),
        name="merge",
    )(*heads, x, *heads_s, xs, w_out, ln_g, ln_b)


def _mlp_part(xb, wu_ref, wd_ref):
    u = jnp.maximum(jnp.dot(xb, wu_ref[...], preferred_element_type=F32), 0.0)
    return jnp.dot((u * u).astype(BF16), wd_ref[...], preferred_element_type=F32)


def _ln_rows(r, g_ref, b_ref):
    mu = jnp.mean(r, axis=-1, keepdims=True)
    d = r - mu
    var = jnp.mean(d * d, axis=-1, keepdims=True)
    return d * lax.rsqrt(var + LN_EPS) * g_ref[...] + b_ref[...]


def _mlp_kernel(x_ref, xs_ref, wu_ref, wd_ref, g_ref, b_ref, o_ref, os_ref, xb_ref, acc_ref, accs_ref,
                *, alpha):
    i, j = pl.program_id(0), pl.program_id(1)
    last = pl.num_programs(1) - 1

    @pl.when(j == 0)
    def _():
        x = x_ref[...]
        xb_ref[...] = x.astype(BF16)
        acc_ref[...] = alpha * x

    acc_ref[...] += _mlp_part(xb_ref[...], wu_ref, wd_ref)

    @pl.when(j == last)
    def _():
        o_ref[...] = _ln_rows(acc_ref[...], g_ref, b_ref)

    @pl.when(i == 0)
    def _():
        xs = xs_ref[...]
        part = _mlp_part(xs.astype(BF16), wu_ref, wd_ref)

        @pl.when(j == 0)
        def _():
            accs_ref[...] = alpha * xs + part

        @pl.when(j > 0)
        def _():
            accs_ref[...] += part

        @pl.when(j == last)
        def _():
            os_ref[...] = _ln_rows(accs_ref[...], g_ref, b_ref)


def _mlp(x, xs, w_up, w_down, layer, ln_g, ln_b, alpha, tf):
    m, d = x.shape
    ms = xs.shape[0]
    d_ff = w_up.shape[2]
    tm = _row_tile(m)
    vec = pl.BlockSpec((1, d), lambda i, j: (0, 0))
    return pl.pallas_call(
        functools.partial(_mlp_kernel, alpha=alpha),
        grid=(m // tm, d_ff // tf),
        in_specs=[pl.BlockSpec((tm, d), lambda i, j: (i, 0)),
                  pl.BlockSpec((ms, d), lambda i, j: (0, 0)),
                  pl.BlockSpec((None, d, tf), lambda i, j: (layer, 0, j)),
                  pl.BlockSpec((None, tf, d), lambda i, j: (layer, j, 0)),
                  vec, vec],
        out_specs=[pl.BlockSpec((tm, d), lambda i, j: (i, 0)),
                   pl.BlockSpec((ms, d), lambda i, j: (0, 0))],
        out_shape=[jax.ShapeDtypeStruct((m, d), F32), jax.ShapeDtypeStruct((ms, d), F32)],
        scratch_shapes=[pltpu.VMEM((tm, d), BF16), pltpu.VMEM((tm, d), F32), pltpu.VMEM((ms, d), F32)],
        compiler_params=_cparams(2),
        name="mlp",
    )(x, xs, w_up, w_down, ln_g, ln_b)


def kernel(x_prompt, x_sample, cache_k, cache_v, page_table, state_conv, state_C, state_n, state_m,
           meta_tokens, w_in, sb_bias, conv_w, b_i, b_f, g_mix, w_out, ln1_g, ln1_b, w_up, w_down,
           ln2_g, ln2_b):
    depth, d_model, d_in = w_in.shape
    bp, n_seq, _ = x_prompt.shape
    bs, n_dec, _ = x_sample.shape
    n_phys, page, h_att = cache_k.shape[1:4]
    assert page == PAGE_SIZE and n_dec < PAGE_SIZE
    h_ml = b_i.shape[1]
    d_att = h_att * HEAD_DIM
    d_conv = conv_w.shape[2]
    d_ml = h_ml * HEAD_DIM
    d_main = 3 * d_att + 3 * d_conv + 4 * d_ml
    assert d_in == d_main + 2 * h_ml and 2 * h_ml <= LANES
    t_p = N_META + n_seq
    alpha = (2 * depth) ** 0.25
    col_conv = 3 * d_att
    col_ml = col_conv + 3 * d_conv
    tn_proj = d_main // 4
    tf = 1024

    w_in_b = jnp.swapaxes(w_in, 1, 2).astype(BF16)
    w_gate = w_in_b[:, d_in - LANES:, :]
    gate_lane = LANES - 2 * h_ml
    w_out_b = w_out.astype(BF16)
    w_up_b = w_up.astype(BF16)
    w_down_b = w_down.astype(BF16)

    pos = jnp.arange(BLK)
    u_strict = (pos[:, None] > pos[None, :]).astype(BF16)
    u_half = jnp.concatenate([u_strict, jnp.ones((BLK, LANES), BF16)], axis=1)
    u2 = jnp.concatenate([u_half, u_half], axis=0)

    row2 = lambda a: a.reshape(1, -1)
    lane_bcast = lambda a: jnp.broadcast_to(a[..., None, None], a.shape + (1, LANES))

    xp = jnp.concatenate([jnp.broadcast_to(meta_tokens[None], (bp, N_META, d_model)), x_prompt],
                         axis=1).reshape(bp * t_p, d_model)
    xs = x_sample.reshape(bs * n_dec, d_model)

    zero_prev = jnp.zeros((bp, SUBLANES, d_conv), F32)
    zero_c = jnp.zeros((bp, h_ml, HEAD_DIM, HEAD_DIM), F32)
    zero_v = jnp.zeros((bp, h_ml, 1, LANES), F32)

    kv_prompt = None
    outs = {name: [] for name in ("cp", "Cp", "np", "mp", "ks", "vs", "cs", "Cs", "ns", "ms")}

    for l in range(depth):
        gm = row2(g_mix[l])
        p, g, p_s, g_s = _proj(xp, xs, w_in_b, w_gate, l, d_main, tn_proj)

        p3 = p.reshape(bp, t_p, d_main)
        g3 = g.reshape(bp, t_p, LANES)
        h_a, k_all, v_all = _attn_prompt(p3, sb_bias[l], u2, gm, kv_prompt, l, depth, N_META)
        kv_prompt = (k_all, v_all)
        y_c, conv_st = _conv(p3, zero_prev, conv_w[l], gm, col_conv, d_att, d_conv, BF16)
        h_m, c_fin, n_fin, m_fin = _mlstm(p3, g3, b_i[l], b_f[l], gm, zero_c, zero_v, zero_v, col_ml,
                                          d_att + d_conv, gate_lane, BF16)
        outs["cp"].append(conv_st)
        outs["Cp"].append(c_fin)
        outs["np"].append(n_fin[:, :, 0, :])
        outs["mp"].append(m_fin[:, :, 0, 0])
        heads_p = (h_a.reshape(bp * t_p, d_att), y_c.reshape(bp * t_p, d_conv),
                   h_m.reshape(bp * t_p, d_ml))

        p3 = p_s.reshape(bs, n_dec, d_main)
        g3 = g_s.reshape(bs, n_dec, LANES)
        q_s = p3[:, :, :d_att].reshape(bs, n_dec, h_att, HEAD_DIM)
        k_s = p3[:, :, d_att:2 * d_att].reshape(bs, n_dec, h_att, HEAD_DIM)
        v_s = p3[:, :, 2 * d_att:3 * d_att].reshape(bs, n_dec, h_att, HEAD_DIM)
        h_a = _attn_sample(q_s, k_s, v_s, cache_k, cache_v, page_table, sb_bias[l], u2, gm, l)
        prev8 = jnp.pad(state_conv[l], ((0, 0), (SUBLANES - (CONV_WIDTH - 1), 0), (0, 0)))
        y_c, conv_st = _conv(p3, prev8, conv_w[l], gm, col_conv, d_att, d_conv, F32)
        h_m, c_fin, n_fin, m_fin = _mlstm(p3, g3, b_i[l], b_f[l], gm, state_C[l],
                                          state_n[l][:, :, None, :], lane_bcast(state_m[l]), col_ml,
                                          d_att + d_conv, gate_lane, F32)
        outs["ks"].append(k_s)
        outs["vs"].append(v_s)
        outs["cs"].append(conv_st)
        outs["Cs"].append(c_fin)
        outs["ns"].append(n_fin[:, :, 0, :])
        outs["ms"].append(m_fin[:, :, 0, 0])
        heads_s = (h_a.reshape(bs * n_dec, d_att), y_c.reshape(bs * n_dec, d_conv),
                   h_m.reshape(bs * n_dec, d_ml))

        xp, xs = _merge(heads_p, xp, heads_s, xs, w_out_b, l, row2(ln1_g[l]), row2(ln1_b[l]), alpha)
        xp, xs = _mlp(xp, xs, w_up_b, w_down_b, l, row2(ln2_g[l]), row2(ln2_b[l]), alpha, tf)

    y_prompt = xp.reshape(bp, t_p, d_model)[:, N_META:]
    y_sample = xs.reshape(bs, n_dec, d_model)
    k_prompt = kv_prompt[0].reshape(depth, bp, t_p, h_att, HEAD_DIM)
    v_prompt = kv_prompt[1].reshape(depth, bp, t_p, h_att, HEAD_DIM)
    st = lambda name: jnp.stack(outs[name])
    return (y_prompt, y_sample, k_prompt, v_prompt, st("cp"), st("Cp"), st("np"), st("mp"),
            st("ks"), st("vs"), st("cs"), st("Cs"), st("ns"), st("ms"))
```

```python
import functools

import jax
import jax.numpy as jnp
from jax import lax
from jax.experimental import pallas as pl
from jax.experimental.pallas import tpu as pltpu

F32 = jnp.float32
BF16 = jnp.bfloat16

HEAD_DIM = 128
N_META = 16
PAGE_SIZE = 128
CONV_WIDTH = 3
LN_EPS = 1e-5
RMS_EPS = 1e-6
SB_SCALE = HEAD_DIM ** -0.5
MLSTM_K_SCALE = HEAD_DIM ** -0.5
NEG_BIG = -1e30

V7X_VMEM_LIMIT_BYTES = 56 * 1024 * 1024
LANES = 128
SUBLANES = 8
BLK = 128


def _cparams(n_axes):
    return pltpu.CompilerParams(dimension_semantics=("arbitrary",) * n_axes,
                                vmem_limit_bytes=V7X_VMEM_LIMIT_BYTES)


def _row_tile(m, cap=704):
    if m <= cap:
        return m
    best = None
    for t in range(16, cap + 1, 16):
        if m % t == 0:
            best = t
    assert best is not None, m
    return best


def _log_sigmoid_pair(z):
    sp = jnp.log(1.0 + jnp.exp(-jnp.abs(z)))
    return jnp.minimum(z, 0.0) - sp, -jnp.maximum(z, 0.0) - sp


def _head_rms(x, g):
    ms = jnp.mean(x * x, axis=-1, keepdims=True)
    return x * lax.rsqrt(ms + RMS_EPS) * g


def _qk(qt, kt):
    return lax.dot_general(qt, kt, (((1,), (1,)), ((), ())), preferred_element_type=F32)


def _split_bf16(x):
    hi = x.astype(BF16)
    return hi, (x - hi.astype(F32)).astype(BF16)


def _proj_kernel(x_ref, xs_ref, w_ref, wg_ref, p_ref, g_ref, ps_ref, gs_ref, xb_ref):
    i, j = pl.program_id(0), pl.program_id(1)

    @pl.when(j == 0)
    def _():
        xb = x_ref[...].astype(BF16)
        xb_ref[...] = xb
        g_ref[...] = _qk(xb, wg_ref[...])

    p_ref[...] = _qk(xb_ref[...], w_ref[...])

    @pl.when(i == 0)
    def _():
        xsb = xs_ref[...].astype(BF16)
        ps_ref[...] = _qk(xsb, w_ref[...])

        @pl.when(j == 0)
        def _():
            gs_ref[...] = _qk(xsb, wg_ref[...])


def _proj(x, xs, w, wg, layer, n, tn):
    m, k = x.shape
    ms = xs.shape[0]
    tm = _row_tile(m)
    n_col = n // tn
    return pl.pallas_call(
        _proj_kernel,
        grid=(m // tm, n_col),
        in_specs=[pl.BlockSpec((tm, k), lambda i, j: (i, 0)),
                  pl.BlockSpec((ms, k), lambda i, j: (0, 0)),
                  pl.BlockSpec((None, tn, k), lambda i, j: (layer, j, 0)),
                  pl.BlockSpec((None, LANES, k), lambda i, j: (layer, 0, 0))],
        out_specs=[pl.BlockSpec((tm, tn), lambda i, j: (i, j)),
                   pl.BlockSpec((tm, LANES), lambda i, j: (i, 0)),
                   pl.BlockSpec((ms, tn), lambda i, j: (0, jnp.where(i == 0, j, n_col - 1))),
                   pl.BlockSpec((ms, LANES), lambda i, j: (0, 0))],
        out_shape=[jax.ShapeDtypeStruct((m, n), F32),
                   jax.ShapeDtypeStruct((m, LANES), F32),
                   jax.ShapeDtypeStruct((ms, n), F32),
                   jax.ShapeDtypeStruct((ms, LANES), F32)],
        scratch_shapes=[pltpu.VMEM((tm, k), BF16)],
        compiler_params=_cparams(2),
        name="proj",
    )(x, xs, w, wg)


def _sb_scores(s, bias, u2, mask):
    tq, nk = s.shape
    m = nk // BLK
    lsz, lk = _log_sigmoid_pair(s * SB_SCALE + bias)
    if mask is not None:
        lk = jnp.where(mask, lk, 0.0)
    hi, lo = _split_bf16(lk)
    lanes = lambda x, j: x[:, j * BLK:(j + 1) * BLK]
    stacked = jnp.concatenate(
        [jnp.concatenate([lanes(hi, j), lanes(lo, j)], axis=1) for j in range(m)], axis=0)
    cs = jnp.dot(stacked, u2, preferred_element_type=F32)
    pieces, later = [None] * m, None
    for j in range(m - 1, -1, -1):
        sfx, tot = cs[j * tq:(j + 1) * tq, :BLK], cs[j * tq:(j + 1) * tq, BLK:]
        pieces[j] = lanes(lsz, j) + (sfx if later is None else sfx + later)
        later = tot if later is None else later + tot
    return (pieces[0] if m == 1 else jnp.concatenate(pieces, axis=1)), later


def _sb_weighted(logit, carry, vt, mask):
    if carry is not None:
        m = logit.shape[1] // BLK
        logit = logit + (carry if m == 1 else jnp.concatenate([carry] * m, axis=1))
    a = jnp.exp(logit)
    if mask is not None:
        a = jnp.where(mask, a, 0.0)
    return jnp.dot(a.astype(BF16), vt, preferred_element_type=F32)


def _attn_kernel(bias_ref, q_ref, k_ref, v_ref, u2_ref, gmix_ref, o_ref, ko_ref, vo_ref,
                 qb, kb, vb, carry_ref, acc_ref, *s_refs, n_lead, run):
    t = q_ref.shape[0]
    tq = carry_ref.shape[0]
    n_run = tq // run
    n_qblk = (t - n_lead) // tq
    bias = bias_ref[pl.program_id(1)]

    qb[...] = q_ref[...].astype(BF16)
    kb[...] = k_ref[...].astype(BF16)
    vb[...] = v_ref[...].astype(BF16)
    ko_ref[...] = k_ref[...]
    vo_ref[...] = v_ref[...]
    u2 = u2_ref[...]

    row = lax.broadcasted_iota(jnp.int32, (BLK, BLK), 0)
    col = lax.broadcasted_iota(jnp.int32, (BLK, BLK), 1)
    logit, _ = _sb_scores(_qk(qb[0:BLK, :], kb[0:BLK, :]), bias, u2, col < row)
    lead = _sb_weighted(logit, None, vb[0:BLK, :], col < row)[0:n_lead]
    o_ref[0:n_lead, :] = _head_rms(lead, gmix_ref[...]).astype(o_ref.dtype)

    def q_block(i, _):
        r0 = pl.multiple_of(n_lead + i * tq, 16)
        qt = qb[pl.ds(r0, tq), :]

        def products(c0):
            return _qk(qt, kb[pl.ds(pl.multiple_of(jnp.maximum(c0, n_lead), 16), run), :])

        def step(s, c0):
            logit, tot = _sb_scores(s, bias, u2, None)
            carry = carry_ref[...]
            acc_ref[...] += _sb_weighted(logit, carry, vb[pl.ds(pl.multiple_of(c0, 16), run), :], None)
            carry_ref[...] = carry + tot

        carry = acc = None
        for d in range(n_run - 1, -1, -1):
            c0 = pl.multiple_of(r0 + d * run, 16)
            mask = (lax.broadcasted_iota(jnp.int32, (tq, run), 1) + d * run
                    < lax.broadcasted_iota(jnp.int32, (tq, run), 0))
            logit, tot = _sb_scores(_qk(qt, kb[pl.ds(c0, run), :]), bias, u2, mask)
            pv = _sb_weighted(logit, carry, vb[pl.ds(c0, run), :], mask)
            acc = pv if acc is None else acc + pv
            carry = tot if carry is None else carry + tot
        carry_ref[...] = carry
        acc_ref[...] = acc

        if n_run % 2 == 0:
            s_even, s_odd = s_refs
            s_even[...] = products(r0 - run)

            def pair(n, _):
                c0 = r0 - (2 * n + 1) * run
                s = s_even[...]
                s_odd[...] = products(c0 - run)
                step(s, c0)
                s = s_odd[...]
                s_even[...] = products(c0 - 2 * run)
                step(s, c0 - run)
                return 0

            lax.fori_loop(0, i * (n_run // 2), pair, 0)
        else:
            def single(n, _):
                c0 = r0 - (n + 1) * run
                step(products(c0), c0)
                return 0

            lax.fori_loop(0, i * n_run, single, 0)

        mask = lax.broadcasted_iota(jnp.int32, (tq, BLK), 1) < n_lead
        logit, _ = _sb_scores(_qk(qt, kb[0:BLK, :]), bias, u2, mask)
        out = acc_ref[...] + _sb_weighted(logit, carry_ref[...], vb[0:BLK, :], mask)
        o_ref[pl.ds(r0, tq), :] = _head_rms(out, gmix_ref[...]).astype(o_ref.dtype)
        return 0

    lax.fori_loop(0, n_qblk, q_block, 0)


def _attn_kernel_aliased(bias_ref, q_ref, k_ref, v_ref, u2_ref, gmix_ref, kprev_ref, vprev_ref,
                         *rest, **kw):
    del kprev_ref, vprev_ref
    _attn_kernel(bias_ref, q_ref, k_ref, v_ref, u2_ref, gmix_ref, *rest, **kw)


def _attn_prompt(p3, sb_bias_l, u2, g_mix_l, kv_prev, layer, depth, n_lead):
    nb, t, _ = p3.shape
    n_heads = sb_bias_l.shape[0]
    assert 0 < n_lead <= BLK <= t and n_lead % 16 == 0
    tq = max(c for c in (512, 256, 128) if (t - n_lead) % c == 0)
    run = min(tq, 256)
    seq = lambda off: pl.BlockSpec((None, t, HEAD_DIM), lambda b, h: (b, 0, off + h))
    kv_spec = pl.BlockSpec((None, None, t, HEAD_DIM), lambda b, h: (layer, b, 0, h))
    kv_shape = jax.ShapeDtypeStruct((depth, nb, t, n_heads * HEAD_DIM), F32)
    in_specs = [pl.BlockSpec(memory_space=pltpu.SMEM), seq(0), seq(n_heads), seq(2 * n_heads),
                pl.BlockSpec((2 * BLK, BLK + LANES), lambda b, h: (0, 0)),
                pl.BlockSpec((1, HEAD_DIM), lambda b, h: (0, h))]
    args = [sb_bias_l, p3, p3, p3, u2, g_mix_l]
    aliases = {}
    body = _attn_kernel
    if kv_prev is not None:
        in_specs += [pl.BlockSpec(memory_space=pl.ANY)] * 2
        args += list(kv_prev)
        aliases = {6: 1, 7: 2}
        body = _attn_kernel_aliased
    return pl.pallas_call(
        functools.partial(body, n_lead=n_lead, run=run),
        grid=(nb, n_heads),
        in_specs=in_specs,
        out_specs=[pl.BlockSpec((None, t, HEAD_DIM), lambda b, h: (b, 0, h)), kv_spec, kv_spec],
        out_shape=[jax.ShapeDtypeStruct((nb, t, n_heads * HEAD_DIM), BF16), kv_shape, kv_shape],
        scratch_shapes=[pltpu.VMEM((t, HEAD_DIM), BF16)] * 3 + [pltpu.VMEM((tq, LANES), F32)] * 2
                       + [pltpu.VMEM((tq, run), F32)] * 2,
        input_output_aliases=aliases,
        compiler_params=_cparams(2),
        name="attn_prompt",
    )(*args)


def _attn_sample_kernel(pt_ref, q_ref, bias_ref, own_ref, new_ref, u2_ref, grow_ref, kn_ref, vn_ref,
                        *refs, n_group):
    del pt_ref
    kc_refs, vc_refs = refs[:n_group], refs[n_group:2 * n_group]
    o_ref, acc_ref, carry_ref = refs[2 * n_group:]
    p = pl.program_id(1)
    n_rows = acc_ref.shape[0]
    n_lanes = own_ref.shape[1]
    n_blk = n_lanes // LANES

    def scores(k_ref, keep):
        k2 = k_ref[...].reshape(n_lanes, HEAD_DIM).astype(BF16)
        s = lax.dot_general(q_ref[...], k2, (((1,), (1,)), ((), ())), preferred_element_type=F32)
        lsz, lk = _log_sigmoid_pair(s * SB_SCALE + bias_ref[...])
        hi, lo = _split_bf16(lk * keep)
        lane_blocks = lambda x: jnp.concatenate(
            [x[:, j * LANES:(j + 1) * LANES] for j in range(n_blk)], axis=0)
        return lsz, jnp.concatenate([lane_blocks(hi), lane_blocks(lo)], axis=1)

    def weighted(lsz, cs, v_ref, keep, acc, carry):
        logits = [None] * n_blk
        for j in range(n_blk - 1, -1, -1):
            rows = slice(j * n_rows, (j + 1) * n_rows)
            logits[j] = cs[rows, :LANES] + carry
            carry = carry + cs[rows, LANES:]
        a = jnp.exp(lsz + jnp.concatenate(logits, axis=1)) * keep
        v2 = v_ref[...].reshape(n_lanes, HEAD_DIM).astype(BF16)
        return acc + jnp.dot(a.astype(BF16), v2, preferred_element_type=F32), carry

    def pages(k_refs, v_refs, keep, acc, carry):
        parts = [scores(k_ref, keep) for k_ref in k_refs]
        cs = jnp.dot(jnp.concatenate([x for _, x in parts], axis=0), u2_ref[...],
                     preferred_element_type=F32)
        per_page = n_blk * n_rows
        for g, v_ref in enumerate(v_refs):
            acc, carry = weighted(parts[g][0], cs[g * per_page:(g + 1) * per_page], v_ref, keep,
                                  acc, carry)
        return acc, carry

    @pl.when(p == 0)
    def _():
        acc, carry = pages([kn_ref], [vn_ref], new_ref[...], jnp.zeros(acc_ref.shape, F32),
                           jnp.zeros(carry_ref.shape, F32))
        acc_ref[...] = acc
        carry_ref[...] = carry

    @pl.when(p > 0)
    def _():
        acc, carry = pages(kc_refs, vc_refs, own_ref[...], acc_ref[...], carry_ref[...])
        acc_ref[...] = acc
        carry_ref[...] = carry

    @pl.when(p == pl.num_programs(1) - 1)
    def _():
        o_ref[...] = _head_rms(acc_ref[...], grow_ref[...])


def _attn_sample(q, k_new, v_new, cache_k, cache_v, page_table, sb_bias_l, u2, g_mix_l, layer):
    nb, n_dec, n_heads, _ = q.shape
    n_pages = page_table.shape[1]
    n_rows = n_heads * n_dec
    n_lanes = PAGE_SIZE * n_heads
    assert n_rows % 16 == 0 and n_heads == SUBLANES
    n_group = max(c for c in (16, 8, 4, 2, 1) if n_pages % c == 0)
    q_rows = jnp.transpose(q, (0, 2, 1, 3)).reshape(nb, n_rows, HEAD_DIM).astype(BF16)
    row_head = jnp.repeat(jnp.arange(n_heads), n_dec)[:, None]
    row_query = jnp.tile(jnp.arange(n_dec), n_heads)[:, None]
    lane_pos = jnp.repeat(jnp.arange(PAGE_SIZE), n_heads)[None, :]
    lane_head = jnp.tile(jnp.arange(n_heads), PAGE_SIZE)[None, :]
    own = (row_head == lane_head).astype(F32)
    own_new = own * (lane_pos < row_query).astype(F32)
    bias_rows = jnp.broadcast_to(jnp.repeat(sb_bias_l, n_dec)[:, None], (n_rows, n_lanes))
    pad_pos = lambda a: jnp.pad(a, ((0, 0), (0, PAGE_SIZE - n_dec), (0, 0), (0, 0)))
    g_rows = jnp.repeat(g_mix_l[0, :n_heads * HEAD_DIM].reshape(n_heads, HEAD_DIM), n_dec, axis=0)

    def cache_spec(g):
        def index_map(b, p, pt):
            logical = n_pages - 1 - (jnp.maximum(p - 1, 0) * n_group + g)
            return (layer, pt[b, logical], 0, 0, 0)
        return pl.BlockSpec((None, None, PAGE_SIZE, n_heads, HEAD_DIM), index_map)

    new_spec = pl.BlockSpec((None, PAGE_SIZE, n_heads, HEAD_DIM), lambda b, p, pt: (b, 0, 0, 0))
    row_const = pl.BlockSpec((n_rows, n_lanes), lambda b, p, pt: (0, 0))
    cache_specs = [cache_spec(g) for g in range(n_group)]
    out = pl.pallas_call(
        functools.partial(_attn_sample_kernel, n_group=n_group),
        grid_spec=pltpu.PrefetchScalarGridSpec(
            num_scalar_prefetch=1,
            grid=(nb, n_pages // n_group + 1),
            in_specs=[pl.BlockSpec((None, n_rows, HEAD_DIM), lambda b, p, pt: (b, 0, 0)),
                      row_const, row_const, row_const,
                      pl.BlockSpec(u2.shape, lambda b, p, pt: (0, 0)),
                      pl.BlockSpec((n_rows, HEAD_DIM), lambda b, p, pt: (0, 0)),
                      new_spec, new_spec] + cache_specs + cache_specs,
            out_specs=pl.BlockSpec((None, n_rows, HEAD_DIM), lambda b, p, pt: (b, 0, 0)),
            scratch_shapes=[pltpu.VMEM((n_rows, HEAD_DIM), F32), pltpu.VMEM((n_rows, LANES), F32)]),
        out_shape=jax.ShapeDtypeStruct((nb, n_rows, HEAD_DIM), F32),
        compiler_params=_cparams(2),
        name="attn_sample",
    )(page_table, q_rows, bias_rows, own, own_new, u2, g_rows, pad_pos(k_new), pad_pos(v_new),
      *([cache_k] * n_group), *([cache_v] * n_group))
    return jnp.transpose(out.reshape(nb, n_heads, n_dec, HEAD_DIM), (0, 2, 1, 3)).reshape(
        nb, n_dec, n_heads * HEAD_DIM)


def _conv_kernel(bc_ref, cc_ref, xc_ref, prev_ref, w_ref, gmix_ref, y_ref, st_ref, ext_ref):
    t = bc_ref.shape[0]
    te = ext_ref.shape[0]
    u = cc_ref[...] * xc_ref[...]
    ext_ref[0:SUBLANES, :] = prev_ref[...]
    ext_ref[te - SUBLANES:te, :] = jnp.zeros((SUBLANES, u.shape[1]), F32)
    ext_ref[SUBLANES:SUBLANES + t, :] = u
    e0 = ext_ref[...]
    e1 = pltpu.roll(e0, 1, 0)
    e2 = pltpu.roll(e0, 2, 0)
    w = w_ref[...]
    lo, hi = SUBLANES, SUBLANES + t
    y = bc_ref[...] * (w[0:1] * e2[lo:hi] + w[1:2] * e1[lo:hi] + w[2:3] * e0[lo:hi])
    for s in range(y.shape[1] // HEAD_DIM):
        sl = slice(s * HEAD_DIM, (s + 1) * HEAD_DIM)
        y_ref[:, sl] = _head_rms(y[:, sl], gmix_ref[:, sl]).astype(y_ref.dtype)
    st_ref[...] = ext_ref[hi - (CONV_WIDTH - 1):hi, :]


def _conv(p3, prev8, conv_w_l, g_mix_l, col0, col_mix, d_conv, out_dtype):
    nb, t, _ = p3.shape
    cb = col0 // d_conv
    te = SUBLANES + pl.cdiv(t, SUBLANES) * SUBLANES
    seq = lambda j: pl.BlockSpec((None, t, d_conv), lambda b: (b, 0, cb + j))
    return pl.pallas_call(
        _conv_kernel,
        grid=(nb,),
        in_specs=[seq(0), seq(1), seq(2),
                  pl.BlockSpec((None, SUBLANES, d_conv), lambda b: (b, 0, 0)),
                  pl.BlockSpec((CONV_WIDTH, d_conv), lambda b: (0, 0)),
                  pl.BlockSpec((1, d_conv), lambda b: (0, col_mix // d_conv))],
        out_specs=[pl.BlockSpec((None, t, d_conv), lambda b: (b, 0, 0)),
                   pl.BlockSpec((None, CONV_WIDTH - 1, d_conv), lambda b: (b, 0, 0))],
        out_shape=[jax.ShapeDtypeStruct((nb, t, d_conv), out_dtype),
                   jax.ShapeDtypeStruct((nb, CONV_WIDTH - 1, d_conv), F32)],
        scratch_shapes=[pltpu.VMEM((te, d_conv), F32)],
        compiler_params=_cparams(1),
        name="conv",
    )(p3, p3, p3, prev8, conv_w_l, g_mix_l)


def _mlstm_kernel(bi_ref, bf_ref, q_ref, k_ref, v_ref, om_ref, g_ref, gmix_ref, c0_ref, n0_ref,
                  m0_ref, h_ref, c_ref, n_ref, m_ref, *pad_refs, gate_lane):
    t = q_ref.shape[0]
    n_heads = c0_ref.shape[0]
    nchunk = pl.cdiv(t, BLK)
    done = (nchunk - 1) * BLK
    srcs = (q_ref, k_ref, v_ref, om_ref, g_ref)
    if t < BLK:
        for src, dst in zip(srcs, pad_refs):
            dst[...] = jnp.zeros(dst.shape, F32)
            dst[0:t, :] = src[...]
        srcs = pad_refs
    qs, ks, vs, oms, gs = srcs
    last_r0 = max(t - BLK, 0)

    row = lax.broadcasted_iota(jnp.int32, (BLK, BLK), 0)
    col = lax.broadcasted_iota(jnp.int32, (BLK, BLK), 1)
    eye = row == col
    causal = col <= row
    tcol = lax.broadcasted_iota(jnp.int32, (BLK, 1), 0)

    def gates(head, r0, valid, g):
        lanes = slice(head * HEAD_DIM, (head + 1) * HEAD_DIM)
        q = qs[pl.ds(r0, BLK), lanes].astype(BF16)
        k = (ks[pl.ds(r0, BLK), lanes] * MLSTM_K_SCALE).astype(BF16)
        im_col = jnp.sum(jnp.where(col == gate_lane + head, g, 0.0), axis=1, keepdims=True)
        fm_col = jnp.sum(jnp.where(col == gate_lane + n_heads + head, g, 0.0), axis=1, keepdims=True)
        ig_col = im_col + bi_ref[head]
        lf_col = _log_sigmoid_pair(fm_col + bf_ref[head])[0]
        if valid is not None:
            ig_col = jnp.where(valid, ig_col, NEG_BIG)
            lf_col = jnp.where(valid, lf_col, 0.0)
        ig_row = jnp.sum(jnp.where(eye, ig_col, 0.0), axis=0, keepdims=True)
        lf_row = jnp.sum(jnp.where(eye, lf_col, 0.0), axis=0, keepdims=True)
        cf_col = jnp.sum(jnp.where(causal, lf_row, 0.0), axis=1, keepdims=True)
        cf_row = jnp.sum(jnp.where(row <= col, lf_col, 0.0), axis=0, keepdims=True)
        log_d = jnp.where(causal, cf_col - cf_row + ig_row, -jnp.inf)
        cf_last = jnp.sum(lf_row, axis=1, keepdims=True)
        log_w = cf_last - cf_row + ig_row
        return dict(q=q, k=k, vf=vs[pl.ds(r0, BLK), lanes], om=oms[pl.ds(r0, BLK), lanes],
                    log_d=log_d, max_d=jnp.max(log_d, axis=1, keepdims=True), qk=_qk(q, k),
                    cf_col=cf_col, cf_last=cf_last, log_w=log_w,
                    max_w=jnp.max(log_w, axis=1, keepdims=True), lanes=lanes)

    def advance(pre, c0, n0, m0):
        q, k, vf = pre["q"], pre["k"], pre["vf"]
        m0 = m0[:, 0:1]
        log_inter = pre["cf_col"] + m0
        m_t = jnp.maximum(log_inter, pre["max_d"])
        w = jnp.exp(pre["log_d"] - m_t) * pre["qk"]
        s_inter = jnp.exp(log_inter - m_t)
        v_ones = jnp.concatenate([vf.astype(BF16), jnp.ones((BLK, LANES), BF16)], axis=1)
        wv = jnp.dot(w.astype(BF16), v_ones, preferred_element_type=F32)
        num = s_inter * _qk(q, c0.astype(BF16)) + wv[:, :HEAD_DIM]
        qn = _qk(q, jnp.broadcast_to(n0, (LANES, HEAD_DIM)).astype(BF16))
        den = s_inter * qn + wv[:, HEAD_DIM:]
        h = num / jnp.maximum(jnp.abs(den), jnp.exp(-m_t))
        gated = jax.nn.sigmoid(pre["om"]) * h

        m_new = jnp.maximum(pre["cf_last"] + m0, pre["max_w"])
        wk_row = jnp.exp(pre["log_w"] - m_new)
        decay = jnp.exp(pre["cf_last"] + m0 - m_new)
        vs_t = (vf.T * wk_row).astype(BF16)
        c_new = decay * c0 + jnp.dot(vs_t, k, preferred_element_type=F32)
        wk_rows = jnp.broadcast_to(wk_row, (2 * SUBLANES, BLK)).astype(BF16)
        n_new = decay * n0 + jnp.dot(wk_rows, k, preferred_element_type=F32)[0:1]
        out = _head_rms(gated, gmix_ref[:, pre["lanes"]]).astype(h_ref.dtype)
        return out, c_new, n_new, jnp.broadcast_to(m_new, (1, LANES))

    def chunk(r0, valid, g, state):
        pres = [gates(head, r0, valid, g) for head in range(n_heads)]
        results = [advance(pres[head], *state[head]) for head in range(n_heads)]
        return [r[0] for r in results], tuple(r[1:] for r in results)

    def body(ci, state):
        r0 = pl.multiple_of(ci * BLK, BLK)
        outs, state = chunk(r0, None, gs[pl.ds(r0, BLK), :], state)
        for head, out in enumerate(outs):
            h_ref[pl.ds(r0, BLK), head * HEAD_DIM:(head + 1) * HEAD_DIM] = out
        return state

    state = tuple((c0_ref[head], n0_ref[head], m0_ref[head]) for head in range(n_heads))
    state = lax.fori_loop(0, nchunk - 1, body, state)
    valid = (tcol + last_r0 >= done) & (tcol + last_r0 < t)
    outs, state = chunk(last_r0, valid, gs[last_r0:last_r0 + BLK, :], state)
    for head, out in enumerate(outs):
        h_ref[done:t, head * HEAD_DIM:(head + 1) * HEAD_DIM] = out[done - last_r0:t - last_r0]
        c_ref[head], n_ref[head], m_ref[head] = state[head]


def _mlstm(p3, g3, b_i_l, b_f_l, g_mix_l, c0, n0, m0, col0, col_mix, gate_lane, out_dtype):
    nb, t, _ = p3.shape
    n_heads = c0.shape[1]
    d_ml = n_heads * HEAD_DIM
    cb = col0 // d_ml
    seq = lambda j: pl.BlockSpec((None, t, d_ml), lambda b: (b, 0, cb + j))
    vec_spec = pl.BlockSpec((None, n_heads, 1, HEAD_DIM), lambda b: (b, 0, 0, 0))
    mat_spec = pl.BlockSpec((None, n_heads, HEAD_DIM, HEAD_DIM), lambda b: (b, 0, 0, 0))
    smem = pl.BlockSpec(memory_space=pltpu.SMEM)
    pads = [pltpu.VMEM((BLK, d_ml), F32)] * 4 + [pltpu.VMEM((BLK, LANES), F32)] if t < BLK else []
    return pl.pallas_call(
        functools.partial(_mlstm_kernel, gate_lane=gate_lane),
        grid=(nb,),
        in_specs=[smem, smem, seq(0), seq(1), seq(2), seq(3),
                  pl.BlockSpec((None, t, LANES), lambda b: (b, 0, 0)),
                  pl.BlockSpec((1, d_ml), lambda b: (0, col_mix // d_ml)),
                  mat_spec, vec_spec, vec_spec],
        out_specs=[pl.BlockSpec((None, t, d_ml), lambda b: (b, 0, 0)),
                   mat_spec, vec_spec, vec_spec],
        out_shape=[jax.ShapeDtypeStruct((nb, t, d_ml), out_dtype),
                   jax.ShapeDtypeStruct(c0.shape, F32),
                   jax.ShapeDtypeStruct(n0.shape, F32),
                   jax.ShapeDtypeStruct(m0.shape, F32)],
        scratch_shapes=pads,
        compiler_params=_cparams(1),
        name="mlstm",
    )(b_i_l, b_f_l, p3, p3, p3, p3, g3, g_mix_l, c0, n0, m0)


def _merge_rows(ha, yc, hm, x, w_ref, g_ref, b_ref, alpha):
    r = alpha * x
    k0 = 0
    for src in (ha, yc, hm):
        k1 = k0 + src.shape[1]
        r = r + jnp.dot(src.astype(BF16), w_ref[k0:k1, :], preferred_element_type=F32)
        k0 = k1
    mu = jnp.mean(r, axis=-1, keepdims=True)
    d = r - mu
    var = jnp.mean(d * d, axis=-1, keepdims=True)
    return d * lax.rsqrt(var + LN_EPS) * g_ref[...] + b_ref[...]


def _merge_kernel(ha_ref, yc_ref, hm_ref, x_ref, has_ref, ycs_ref, hms_ref, xs_ref, w_ref, g_ref, b_ref,
                  o_ref, os_ref, *, alpha):
    o_ref[...] = _merge_rows(ha_ref[...], yc_ref[...], hm_ref[...], x_ref[...], w_ref, g_ref, b_ref,
                             alpha)

    @pl.when(pl.program_id(0) == 0)
    def _():
        os_ref[...] = _merge_rows(has_ref[...], ycs_ref[...], hms_ref[...], xs_ref[...], w_ref, g_ref,
                                  b_ref, alpha)


def _merge(heads, x, heads_s, xs, w_out, layer, ln_g, ln_b, alpha):
    m, d = x.shape
    ms = xs.shape[0]
    d_mix = w_out.shape[1]
    tm = _row_tile(m)
    rows = lambda a: pl.BlockSpec((tm, a.shape[1]), lambda i: (i, 0))
    whole = lambda a: pl.BlockSpec(a.shape, lambda i: (0, 0))
    return pl.pallas_call(
        functools.partial(_merge_kernel, alpha=alpha),
        grid=(m // tm,),
        in_specs=[rows(a) for a in heads] + [rows(x)] + [whole(a) for a in heads_s] + [whole(xs)]
                 + [pl.BlockSpec((None, d_mix, d), lambda i: (layer, 0, 0), pipeline_mode=pl.Buffered(1)),
                    whole(ln_g), whole(ln_b)],
        out_specs=[pl.BlockSpec((tm, d), lambda i: (i, 0)), pl.BlockSpec((ms, d), lambda i: (0, 0))],
        out_shape=[jax.ShapeDtypeStruct((m, d), F32), jax.ShapeDtypeStruct((ms, d), F32)],
        compiler_params=_cparams(1),
        name="merge",
    )(*heads, x, *heads_s, xs, w_out, ln_g, ln_b)


def _mlp_part(xb, wu_ref, wd_ref):
    u = jnp.maximum(jnp.dot(xb, wu_ref[...], preferred_element_type=F32), 0.0)
    return jnp.dot((u * u).astype(BF16), wd_ref[...], preferred_element_type=F32)


def _ln_rows(r, g_ref, b_ref):
    mu = jnp.mean(r, axis=-1, keepdims=True)
    d = r - mu
    var = jnp.mean(d * d, axis=-1, keepdims=True)
    return d * lax.rsqrt(var + LN_EPS) * g_ref[...] + b_ref[...]


def _mlp_kernel(x_ref, xs_ref, wu_ref, wd_ref, g_ref, b_ref, o_ref, os_ref, xb_ref, acc_ref, accs_ref,
                *, alpha):
    i, j = pl.program_id(0), pl.program_id(1)
    last = pl.num_programs(1) - 1

    @pl.when(j == 0)
    def _():
        x = x_ref[...]
        xb_ref[...] = x.astype(BF16)
        acc_ref[...] = alpha * x

    acc_ref[...] += _mlp_part(xb_ref[...], wu_ref, wd_ref)

    @pl.when(j == last)
    def _():
        o_ref[...] = _ln_rows(acc_ref[...], g_ref, b_ref)

    @pl.when(i == 0)
    def _():
        xs = xs_ref[...]
        part = _mlp_part(xs.astype(BF16), wu_ref, wd_ref)

        @pl.when(j == 0)
        def _():
            accs_ref[...] = alpha * xs + part

        @pl.when(j > 0)
        def _():
            accs_ref[...] += part

        @pl.when(j == last)
        def _():
            os_ref[...] = _ln_rows(accs_ref[...], g_ref, b_ref)


def _mlp(x, xs, w_up, w_down, layer, ln_g, ln_b, alpha, tf):
    m, d = x.shape
    ms = xs.shape[0]
    d_ff = w_up.shape[2]
    tm = _row_tile(m)
    vec = pl.BlockSpec((1, d), lambda i, j: (0, 0))
    return pl.pallas_call(
        functools.partial(_mlp_kernel, alpha=alpha),
        grid=(m // tm, d_ff // tf),
        in_specs=[pl.BlockSpec((tm, d), lambda i, j: (i, 0)),
                  pl.BlockSpec((ms, d), lambda i, j: (0, 0)),
                  pl.BlockSpec((None, d, tf), lambda i, j: (layer, 0, j)),
                  pl.BlockSpec((None, tf, d), lambda i, j: (layer, j, 0)),
                  vec, vec],
        out_specs=[pl.BlockSpec((tm, d), lambda i, j: (i, 0)),
                   pl.BlockSpec((ms, d), lambda i, j: (0, 0))],
        out_shape=[jax.ShapeDtypeStruct((m, d), F32), jax.ShapeDtypeStruct((ms, d), F32)],
        scratch_shapes=[pltpu.VMEM((tm, d), BF16), pltpu.VMEM((tm, d), F32), pltpu.VMEM((ms, d), F32)],
        compiler_params=_cparams(2),
        name="mlp",
    )(x, xs, w_up, w_down, ln_g, ln_b)


def kernel(x_prompt, x_sample, cache_k, cache_v, page_table, state_conv, state_C, state_n, state_m,
           meta_tokens, w_in, sb_bias, conv_w, b_i, b_f, g_mix, w_out, ln1_g, ln1_b, w_up, w_down,
           ln2_g, ln2_b):
    depth, d_model, d_in = w_in.shape
    bp, n_seq, _ = x_prompt.shape
    bs, n_dec, _ = x_sample.shape
    n_phys, page, h_att = cache_k.shape[1:4]
    assert page == PAGE_SIZE and n_dec < PAGE_SIZE
    h_ml = b_i.shape[1]
    d_att = h_att * HEAD_DIM
    d_conv = conv_w.shape[2]
    d_ml = h_ml * HEAD_DIM
    d_main = 3 * d_att + 3 * d_conv + 4 * d_ml
    assert d_in == d_main + 2 * h_ml and 2 * h_ml <= LANES
    t_p = N_META + n_seq
    alpha = (2 * depth) ** 0.25
    col_conv = 3 * d_att
    col_ml = col_conv + 3 * d_conv
    tn_proj = d_main // 4
    tf = 1024

    w_in_b = jnp.swapaxes(w_in, 1, 2).astype(BF16)
    w_gate = w_in_b[:, d_in - LANES:, :]
    gate_lane = LANES - 2 * h_ml
    w_out_b = w_out.astype(BF16)
    w_up_b = w_up.astype(BF16)
    w_down_b = w_down.astype(BF16)

    pos = jnp.arange(BLK)
    u_strict = (pos[:, None] > pos[None, :]).astype(BF16)
    u_half = jnp.concatenate([u_strict, jnp.ones((BLK, LANES), BF16)], axis=1)
    u2 = jnp.concatenate([u_half, u_half], axis=0)

    row2 = lambda a: a.reshape(1, -1)
    lane_bcast = lambda a: jnp.broadcast_to(a[..., None, None], a.shape + (1, LANES))

    xp = jnp.concatenate([jnp.broadcast_to(meta_tokens[None], (bp, N_META, d_model)), x_prompt],
                         axis=1).reshape(bp * t_p, d_model)
    xs = x_sample.reshape(bs * n_dec, d_model)

    zero_prev = jnp.zeros((bp, SUBLANES, d_conv), F32)
    zero_c = jnp.zeros((bp, h_ml, HEAD_DIM, HEAD_DIM), F32)
    zero_v = jnp.zeros((bp, h_ml, 1, LANES), F32)

    kv_prompt = None
    outs = {name: [] for name in ("cp", "Cp", "np", "mp", "ks", "vs", "cs", "Cs", "ns", "ms")}

    for l in range(depth):
        gm = row2(g_mix[l])
        p, g, p_s, g_s = _proj(xp, xs, w_in_b, w_gate, l, d_main, tn_proj)

        p3 = p.reshape(bp, t_p, d_main)
        g3 = g.reshape(bp, t_p, LANES)
        h_a, k_all, v_all = _attn_prompt(p3, sb_bias[l], u2, gm, kv_prompt, l, depth, N_META)
        kv_prompt = (k_all, v_all)
        y_c, conv_st = _conv(p3, zero_prev, conv_w[l], gm, col_conv, d_att, d_conv, BF16)
        h_m, c_fin, n_fin, m_fin = _mlstm(p3, g3, b_i[l], b_f[l], gm, zero_c, zero_v, zero_v, col_ml,
                                          d_att + d_conv, gate_lane, BF16)
        outs["cp"].append(conv_st)
        outs["Cp"].append(c_fin)
        outs["np"].append(n_fin[:, :, 0, :])
        outs["mp"].append(m_fin[:, :, 0, 0])
        heads_p = (h_a.reshape(bp * t_p, d_att), y_c.reshape(bp * t_p, d_conv),
                   h_m.reshape(bp * t_p, d_ml))

        p3 = p_s.reshape(bs, n_dec, d_main)
        g3 = g_s.reshape(bs, n_dec, LANES)
        q_s = p3[:, :, :d_att].reshape(bs, n_dec, h_att, HEAD_DIM)
        k_s = p3[:, :, d_att:2 * d_att].reshape(bs, n_dec, h_att, HEAD_DIM)
        v_s = p3[:, :, 2 * d_att:3 * d_att].reshape(bs, n_dec, h_att, HEAD_DIM)
        h_a = _attn_sample(q_s, k_s, v_s, cache_k, cache_v, page_table, sb_bias[l], u2, gm, l)
        prev8 = jnp.pad(state_conv[l], ((0, 0), (SUBLANES - (CONV_WIDTH - 1), 0), (0, 0)))
        y_c, conv_st = _conv(p3, prev8, conv_w[l], gm, col_conv, d_att, d_conv, F32)
        h_m, c_fin, n_fin, m_fin = _mlstm(p3, g3, b_i[l], b_f[l], gm, state_C[l],
                                          state_n[l][:, :, None, :], lane_bcast(state_m[l]), col_ml,
                                          d_att + d_conv, gate_lane, F32)
        outs["ks"].append(k_s)
        outs["vs"].append(v_s)
        outs["cs"].append(conv_st)
        outs["Cs"].append(c_fin)
        outs["ns"].append(n_fin[:, :, 0, :])
        outs["ms"].append(m_fin[:, :, 0, 0])
        heads_s = (h_a.reshape(bs * n_dec, d_att), y_c.reshape(bs * n_dec, d_conv),
                   h_m.reshape(bs * n_dec, d_ml))

        xp, xs = _merge(heads_p, xp, heads_s, xs, w_out_b, l, row2(ln1_g[l]), row2(ln1_b[l]), alpha)
        xp, xs = _mlp(xp, xs, w_up_b, w_down_b, l, row2(ln2_g[l]), row2(ln2_b[l]), alpha, tf)

    y_prompt = xp.reshape(bp, t_p, d_model)[:, N_META:]
    y_sample = xs.reshape(bs, n_dec, d_model)
    k_prompt = kv_prompt[0].reshape(depth, bp, t_p, h_att, HEAD_DIM)
    v_prompt = kv_prompt[1].reshape(depth, bp, t_p, h_att, HEAD_DIM)
    st = lambda name: jnp.stack(outs[name])
    return (y_prompt, y_sample, k_prompt, v_prompt, st("cp"), st("Cp"), st("np"), st("mp"),
            st("ks"), st("vs"), st("cs"), st("Cs"), st("ns"), st("ms"))
```
